```python
import jax, jax.numpy as jnp
from jax import lax
import numpy as np

D_MODEL = 1024
BATCH = 1
SEQ = 16384
DEPTH = 1
DEC_BATCH = 32
DEC_SEQ = 4
PAST_LEN = 16384
PAGE_SIZE = 128

NSA_HEAD_DIM = 64
NSA_HEADS = D_MODEL // 128
NSA_KV_HEADS = 2
NSA_GROUP = NSA_HEADS // NSA_KV_HEADS
NSA_WIDTH = NSA_HEADS * NSA_HEAD_DIM
NSA_KV_WIDTH = NSA_KV_HEADS * NSA_HEAD_DIM
CMP_STRIDE = 16
CMP_BLOCK = 2 * CMP_STRIDE
CMP_HIDDEN = 2 * NSA_HEAD_DIM
SEL_BLOCK = 64
N_SEL = 16
WINDOW = 512
N_BRANCH = 3
FORCE_BONUS = 1.0e4
Q_BLOCK = 128
MLSTM_HEAD_DIM = 128
MLSTM_HEADS = D_MODEL // 256
MLSTM_WIDTH = MLSTM_HEADS * MLSTM_HEAD_DIM
CONV_WIDTH = 4
MLSTM_CHUNK = 64
MIX_WIDTH = NSA_WIDTH + MLSTM_WIDTH
N_MEM = 256
CROSS_HEADS = 4
CROSS_HEAD_DIM = 128
N_GROUPS = 4
EXPERTS_PER_GROUP = 8
TOP_K_IN_GROUP = 2
D_EXPERT = D_MODEL // 4
EPS = 1e-6
IN_SIZES = (NSA_WIDTH, NSA_KV_WIDTH, NSA_KV_WIDTH, NSA_KV_WIDTH, NSA_KV_WIDTH, NSA_KV_WIDTH, NSA_KV_WIDTH,
            NSA_HEADS * N_BRANCH, MLSTM_WIDTH, MLSTM_WIDTH, MLSTM_WIDTH, MLSTM_HEADS, MLSTM_HEADS)
D_IN = sum(IN_SIZES)
F32 = jnp.float32

kernel_name = 'nsa_mlstm_hmoe_hybrid_step'


def rmsnorm(x, g):
    xf = x.astype(F32)
    y = xf * lax.rsqrt(jnp.mean(xf * xf, axis=-1, keepdims=True) + EPS)
    return (y * g.astype(F32)).astype(x.dtype)


def masked_softmax(s, mask):
    s = jnp.where(mask, s, -jnp.inf)
    m = jnp.max(s, axis=-1, keepdims=True)
    m = jnp.where(jnp.isfinite(m), m, 0.0)
    e = jnp.where(mask, jnp.exp(s - m), 0.0)
    den = jnp.sum(e, axis=-1, keepdims=True)
    return e / jnp.where(den > 0, den, 1.0)


def alibi_slopes(n):
    return 2.0 ** (-8.0 * jnp.arange(1, n + 1, dtype=F32) / n)


def in_projection(x, lw):
    h = rmsnorm(x, lw['g_mix']).astype(F32)
    proj = h @ lw['w_in'].astype(F32) + lw['b_in'].astype(F32)
    offs, acc = [], 0
    for s in IN_SIZES[:-1]:
        acc += s
        offs.append(acc)
    q, kc, vc, ks, vs, kw, vw, gt, mu, mv, mo, mi, mf = jnp.split(proj, offs, axis=-1)
    b, t = x.shape[:2]
    kvh = lambda a: a.reshape(b, t, NSA_KV_HEADS, NSA_HEAD_DIM)
    return (q.reshape(b, t, NSA_HEADS, NSA_HEAD_DIM),
            jax.nn.sigmoid(gt).reshape(b, t, NSA_HEADS, N_BRANCH),
            kvh(kc), kvh(vc), kvh(ks), kvh(vs), kvh(kw), kvh(vw), mu, mv, mo, mi, mf)


def compress(kx, pe, w1, b1, w2):
    b, t, h, d = kx.shape
    nc = t // CMP_STRIDE
    xf = kx[:, :nc * CMP_STRIDE].astype(F32).reshape(b, nc, CMP_STRIDE, h, d)
    pe = pe.astype(F32)
    w1 = w1.astype(F32)
    first = xf[:, :-1] + pe[None, None, :CMP_STRIDE, None, :]
    second = xf[:, 1:] + pe[None, None, CMP_STRIDE:, None, :]
    z = (jnp.einsum('bnphd,pdf->bnhf', first, w1[:CMP_STRIDE])
         + jnp.einsum('bnphd,pdf->bnhf', second, w1[CMP_STRIDE:]) + b1.astype(F32))
    return jnp.einsum('bnhf,fd->bnhd', jax.nn.gelu(z), w2.astype(F32))


def compress_kv(k_raw, v_raw, lw):
    kc = compress(k_raw, lw['ck_pe'], lw['ck_w1'], lw['ck_b1'], lw['ck_w2'])
    vc = compress(v_raw, lw['cv_pe'], lw['cv_w1'], lw['cv_b1'], lw['cv_w2'])
    cpos = jnp.arange(kc.shape[1]) * CMP_STRIDE + CMP_BLOCK - 1
    return kc, vc, cpos


def to_sel_blocks(kx):
    b, t, h, d = kx.shape
    ns = -(-t // SEL_BLOCK)
    kx = jnp.pad(kx.astype(F32), ((0, 0), (0, ns * SEL_BLOCK - t), (0, 0), (0, 0)))
    return kx.reshape(b, ns, SEL_BLOCK, h, d).transpose(0, 3, 1, 2, 4)


def gather_blocks(blocks, idx):
    return jax.vmap(jax.vmap(lambda bl, ix: bl[ix]))(blocks, idx)


def nsa_attend(q, gates, qpos, kc, vc, cpos, ksb, vsb, kw, vw, wpos):
    b, tq = q.shape[:2]
    kvh, g, dh = NSA_KV_HEADS, NSA_GROUP, NSA_HEAD_DIM
    qg = q.astype(F32).reshape(b, tq, kvh, g, dh) * dh ** -0.5
    slopes = alibi_slopes(NSA_HEADS).reshape(kvh, g)
    dist_c = qpos[:, None] - cpos[None, :]
    s_c = jnp.einsum('bqhgd,bchd->bhgqc', qg, kc) - slopes[:, :, None, None] * dist_c.astype(F32)
    p_cmp = masked_softmax(s_c, dist_c >= 0)
    o_cmp = jnp.einsum('bhgqc,bchd->bqhgd', p_cmp, vc)
    ns = ksb.shape[2]
    imp = p_cmp.sum(axis=2)
    nc = imp.shape[-1]
    ratio = SEL_BLOCK // CMP_STRIDE
    off = CMP_BLOCK // CMP_STRIDE - 1
    imp = jnp.pad(imp, ((0, 0), (0, 0), (0, 0), (off, ratio * ns - nc)))
    imp_slc = imp[..., 0:ratio * ns:ratio]
    for r in range(1, ratio + off):
        imp_slc = imp_slc + imp[..., r:r + ratio * ns:ratio]
    blk = jnp.arange(ns)
    valid = blk[None, :] * SEL_BLOCK <= qpos[:, None]
    cur = qpos[:, None] // SEL_BLOCK
    forced = (blk[None, :] == 0) | (blk[None, :] == cur) | (blk[None, :] == cur - 1)
    score = jnp.where(valid, imp_slc + jnp.where(forced, FORCE_BONUS, 0.0), -1.0)
    n_top = min(N_SEL, ns)
    _, idx = lax.top_k(score, n_top)
    gk = gather_blocks(ksb, idx)
    gv = gather_blocks(vsb, idx)
    kpos = idx[..., None] * SEL_BLOCK + jnp.arange(SEL_BLOCK)
    dist_s = qpos[:, None, None] - kpos
    s_s = jnp.einsum('bqhgd,bhqksd->bhgqks', qg, gk)
    s_s = s_s - slopes[None, :, :, None, None, None] * dist_s[:, :, None].astype(F32)
    m_sel = n_top * SEL_BLOCK
    p_sel = masked_softmax(s_s.reshape(b, kvh, g, tq, m_sel), (dist_s >= 0)[:, :, None].reshape(b, kvh, 1, tq, m_sel))
    o_sel = jnp.einsum('bhgqm,bhqmd->bqhgd', p_sel, gv.reshape(b, kvh, tq, m_sel, dh))
    dist_w = qpos[:, None] - wpos[None, :]
    s_w = jnp.einsum('bqhgd,bwhd->bhgqw', qg, kw) - slopes[:, :, None, None] * dist_w.astype(F32)
    p_w = masked_softmax(s_w, (dist_w >= 0) & (dist_w < WINDOW) & (wpos[None, :] >= 0))
    o_win = jnp.einsum('bhgqw,bwhd->bqhgd', p_w, kw * 0.0 + vw) if False else jnp.einsum('bhgqw,bwhd->bqhgd', p_w, vw)
    gg = gates.astype(F32).reshape(b, tq, kvh, g, N_BRANCH)
    o = gg[..., 0:1] * o_cmp + gg[..., 1:2] * o_sel + gg[..., 2:3] * o_win
    return o.reshape(b, tq, NSA_WIDTH)


def nsa_prompt(q, gates, kc_raw, vc_raw, ks_raw, vs_raw, kw_raw, vw_raw, lw):
    b, t = q.shape[:2]
    kc, vc, cpos = compress_kv(kc_raw, vc_raw, lw)
    ksb, vsb = to_sel_blocks(ks_raw), to_sel_blocks(vs_raw)
    pad = ((0, 0), (WINDOW, 0), (0, 0), (0, 0))
    kw_pad, vw_pad = jnp.pad(kw_raw, pad), jnp.pad(vw_raw, pad)
    nqb = t // Q_BLOCK
    qb = q.reshape(b, nqb, Q_BLOCK, NSA_HEADS, NSA_HEAD_DIM).swapaxes(0, 1)
    gb = gates.reshape(b, nqb, Q_BLOCK, NSA_HEADS, N_BRANCH).swapaxes(0, 1)

    def one_block(args):
        qi, gi, i = args
        start = i * Q_BLOCK
        qpos = start + jnp.arange(Q_BLOCK)
        kwi = lax.dynamic_slice_in_dim(kw_pad, start, WINDOW + Q_BLOCK, axis=1)
        vwi = lax.dynamic_slice_in_dim(vw_pad, start, WINDOW + Q_BLOCK, axis=1)
        wpos = start - WINDOW + jnp.arange(WINDOW + Q_BLOCK)
        return nsa_attend(qi, gi, qpos, kc, vc, cpos, ksb, vsb, kwi, vwi, wpos)

    out = lax.map(one_block, (qb, gb, jnp.arange(nqb)))
    return out.swapaxes(0, 1).reshape(b, t, NSA_WIDTH)


def mlstm_chunk(carry, inp):
    c_prev, n_prev, m_prev = carry
    q, k, v, ig, lf = inp
    lc = q.shape[2]
    bcum = jnp.cumsum(lf, axis=-1)
    causal = jnp.tril(jnp.ones((lc, lc), dtype=bool))
    dlog = jnp.where(causal, bcum[..., :, None] - bcum[..., None, :] + ig[..., None, :], -jnp.inf)
    inter = bcum + m_prev[..., None]
    m_t = jnp.maximum(inter, jnp.max(dlog, axis=-1))
    w_intra = jnp.exp(dlog - m_t[..., None])
    w_inter = jnp.exp(inter - m_t)
    qk = jnp.einsum('bhtd,bhsd->bhts', q, k) * w_intra
    num = w_inter[..., None] * jnp.einsum('bhtd,bhde->bhte', q, c_prev) + jnp.einsum('bhts,bhse->bhte', qk, v)
    den = w_inter * jnp.einsum('bhtd,bhd->bht', q, n_prev) + qk.sum(-1)
    h = num / jnp.maximum(jnp.abs(den), jnp.exp(-m_t))[..., None]
    b_last = bcum[..., -1]
    dl_end = b_last[..., None] - bcum + ig
    m_new = jnp.maximum(b_last + m_prev, jnp.max(dl_end, axis=-1))
    a_prev = jnp.exp(b_last + m_prev - m_new)
    w_end = jnp.exp(dl_end - m_new[..., None])
    c_new = a_prev[..., None, None] * c_prev + jnp.einsum('bhs,bhsd,bhse->bhde', w_end, k, v)
    n_new = a_prev[..., None] * n_prev + jnp.einsum('bhs,bhsd->bhd', w_end, k)
    return (c_new, n_new, m_new), h


def mlstm_mix(mu, mv, mo, mi, mf, conv0, c0, n0, m0, lw):
    b, t, _ = mu.shape
    ext = jnp.concatenate([conv0.astype(F32), mu], axis=1)
    cw = lw['conv_w'].astype(F32)
    conv = lw['conv_b'].astype(F32) + ext[:, 0:t] * cw[0]
    for j in range(1, CONV_WIDTH):
        conv = conv + ext[:, j:j + t] * cw[j]
    c = jax.nn.silu(conv).reshape(b, t, MLSTM_HEADS, MLSTM_HEAD_DIM)
    q = jnp.einsum('bthd,hde->bhte', c, lw['m_wq'].astype(F32))
    k = jnp.einsum('bthd,hde->bhte', c, lw['m_wk'].astype(F32)) * MLSTM_HEAD_DIM ** -0.5
    v = mv.reshape(b, t, MLSTM_HEADS, MLSTM_HEAD_DIM).transpose(0, 2, 1, 3)
    ig = mi.transpose(0, 2, 1)
    lf = jax.nn.log_sigmoid(mf).transpose(0, 2, 1)
    lc = MLSTM_CHUNK if t % MLSTM_CHUNK == 0 else t
    nch = t // lc
    chunks = lambda a: jnp.moveaxis(a.reshape(a.shape[:2] + (nch, lc) + a.shape[3:]), 2, 0)
    (c1, n1, m1), hs = lax.scan(mlstm_chunk, (c0.astype(F32), n0.astype(F32), m0.astype(F32)),
                                (chunks(q), chunks(k), chunks(v), chunks(ig), chunks(lf)))
    h = jnp.moveaxis(hs, 0, 2).reshape(b, MLSTM_HEADS, t, MLSTM_HEAD_DIM).transpose(0, 2, 1, 3)
    hn = h * lax.rsqrt(jnp.mean(h * h, axis=-1, keepdims=True) + EPS)
    hn = hn * lw['m_norm_g'].astype(F32).reshape(MLSTM_HEADS, MLSTM_HEAD_DIM)
    out = jax.nn.sigmoid(mo) * hn.reshape(b, t, MLSTM_WIDTH)
    return out, c1, n1, m1, ext[:, t:]


def mem_kv(mem, lw):
    b = mem.shape[0]
    mn = rmsnorm(mem, lw['g_mem']).astype(F32)
    mk = (mn @ lw['w_ck'].astype(F32)).reshape(b, N_MEM, CROSS_HEADS, CROSS_HEAD_DIM)
    mv = (mn @ lw['w_cv'].astype(F32)).reshape(b, N_MEM, CROSS_HEADS, CROSS_HEAD_DIM)
    return mk, mv


def cross_attn(h, mk, mv, w_cq, w_co):
    b, t, _ = h.shape
    q = (h.astype(F32) @ w_cq.astype(F32)).reshape(b, t, CROSS_HEADS, CROSS_HEAD_DIM) * CROSS_HEAD_DIM ** -0.5
    p = jax.nn.softmax(jnp.einsum('bthd,bshd->bhts', q, mk.astype(F32)), axis=-1)
    o = jnp.einsum('bhts,bshd->bthd', p, mv.astype(F32)).reshape(b, t, CROSS_HEADS * CROSS_HEAD_DIM)
    return o @ w_co.astype(F32)


def hmoe(h, lw):
    hf = h.astype(F32)
    b, t, _ = hf.shape
    p_group = jax.nn.softmax(hf @ lw['w_group'].astype(F32) + lw['b_group'].astype(F32), axis=-1)
    pg_top, g_idx = lax.top_k(p_group, 1)
    logits = (hf @ lw['w_expert'].astype(F32) + lw['b_expert'].astype(F32)).reshape(b, t, N_GROUPS, EXPERTS_PER_GROUP)
    lg = jnp.take_along_axis(logits, g_idx[..., None], axis=2)[:, :, 0]
    top_w, top_i = lax.top_k(jax.nn.softmax(lg, axis=-1), TOP_K_IN_GROUP)
    top_w = top_w / jnp.sum(top_w, axis=-1, keepdims=True)
    local = jnp.sum(jax.nn.one_hot(top_i, EXPERTS_PER_GROUP, dtype=F32) * top_w[..., None], axis=2)
    combine = jax.nn.one_hot(g_idx[..., 0], N_GROUPS, dtype=F32)[..., None] * (pg_top[..., None] * local[:, :, None, :])
    y = jnp.zeros_like(hf)
    for gi in range(N_GROUPS):
        a = jnp.einsum('btd,edf->btef', hf, lw['e_w1'][gi].astype(F32))
        u = jnp.einsum('btd,edf->btef', hf, lw['e_w3'][gi].astype(F32))
        hid = jax.nn.silu(a) * u * combine[:, :, gi, :, None]
        y = y + jnp.einsum('btef,efd->btd', hid, lw['e_w2'][gi].astype(F32))
    return y


def finish_layer(x, nsa_out, ml_out, mk, mv, lw):
    mix = jnp.concatenate([nsa_out, ml_out], axis=-1) @ lw['w_out'].astype(F32)
    x = x + mix.astype(x.dtype)
    x = x + cross_attn(rmsnorm(x, lw['g_cross']), mk, mv, lw['w_cq'], lw['w_co']).astype(x.dtype)
    x = x + hmoe(rmsnorm(x, lw['g_ffn']), lw).astype(x.dtype)
    return x


def prompt_layer(x, mem, lw):
    b, t, _ = x.shape
    q, gates, kc, vc, ks, vs, kw, vw, mu, mv, mo, mi, mf = in_projection(x, lw)
    nsa = nsa_prompt(q, gates, kc, vc, ks, vs, kw, vw, lw)
    conv0 = jnp.zeros((b, CONV_WIDTH - 1, MLSTM_WIDTH), F32)
    c0 = jnp.zeros((b, MLSTM_HEADS, MLSTM_HEAD_DIM, MLSTM_HEAD_DIM), F32)
    n0 = jnp.zeros((b, MLSTM_HEADS, MLSTM_HEAD_DIM), F32)
    m0 = jnp.zeros((b, MLSTM_HEADS), F32)
    ml, c1, n1, m1, conv1 = mlstm_mix(mu, mv, mo, mi, mf, conv0, c0, n0, m0, lw)
    mk, mvv = mem_kv(mem, lw)
    x = finish_layer(x, nsa, ml, mk, mvv, lw)
    wb = min(WINDOW, t)
    return x, (kc, vc, ks, vs, kw[:, t - wb:], vw[:, t - wb:], c1, n1, m1, conv1, mk, mvv)


def sample_layer(x, ck_pool, cv_pool, sk_pool, sv_pool, win_k, win_v, c0, n0, m0, conv0, mem_k, mem_v, page_table, lw):
    b, t, _ = x.shape
    q, gates, kc, vc, ks, vs, kw, vw, mu, mv, mo, mi, mf = in_projection(x, lw)
    past = page_table.shape[1] * PAGE_SIZE

    def with_past(pool, new):
        old = pool[page_table].reshape(b, past, NSA_KV_HEADS, NSA_HEAD_DIM).astype(F32)
        return jnp.concatenate([old, new], axis=1)

    kc_c, vc_c, cpos = compress_kv(with_past(ck_pool, kc), with_past(cv_pool, vc), lw)
    ksb, vsb = to_sel_blocks(with_past(sk_pool, ks)), to_sel_blocks(with_past(sv_pool, vs))
    wb = win_k.shape[1]
    kw_all = jnp.concatenate([win_k.astype(F32), kw], axis=1)
    vw_all = jnp.concatenate([win_v.astype(F32), vw], axis=1)
    wpos = past - wb + jnp.arange(wb + t)
    qpos = past + jnp.arange(t)
    nsa = nsa_attend(q, gates, qpos, kc_c, vc_c, cpos, ksb, vsb, kw_all, vw_all, wpos)
    ml, c1, n1, m1, conv1 = mlstm_mix(mu, mv, mo, mi, mf, conv0, c0, n0, m0, lw)
    x = finish_layer(x, nsa, ml, mem_k, mem_v, lw)
    return x, (kc, vc, ks, vs, kw_all[:, t:], vw_all[:, t:], c1, n1, m1, conv1)


def setup_inputs(seed: int = 0) -> dict:
    key = jax.random.key(seed)
    keys = iter(jax.random.split(key, 64))

    def nrm(shape, scale=1.0):
        return jax.random.normal(next(keys), shape, F32) * scale

    def gain(shape):
        return 1.0 + nrm(shape, 0.1)

    d = D_MODEL
    n_pages = PAST_LEN // PAGE_SIZE
    n_used = DEC_BATCH * n_pages
    n_pool = n_used + max(1, n_used // 4)
    page_table = jax.random.permutation(next(keys), n_pool)[:n_used].reshape(DEC_BATCH, n_pages).astype(jnp.int32)
    pool = (DEPTH, n_pool, PAGE_SIZE, NSA_KV_HEADS, NSA_HEAD_DIM)
    win = (DEPTH, DEC_BATCH, min(WINDOW, PAST_LEN), NSA_KV_HEADS, NSA_HEAD_DIM)
    memc = (DEPTH, DEC_BATCH, N_MEM, CROSS_HEADS, CROSS_HEAD_DIM)
    b_in = nrm((DEPTH, D_IN), 0.01).at[:, D_IN - MLSTM_HEADS:].add(jnp.linspace(3.0, 6.0, MLSTM_HEADS))
    cw = CROSS_HEADS * CROSS_HEAD_DIM
    ge = (DEPTH, N_GROUPS, EXPERTS_PER_GROUP)
    return {
        'x_prompt': nrm((BATCH, SEQ, d)),
        'x_sample': nrm((DEC_BATCH, DEC_SEQ, d)),
        'cache_cmp_k': nrm(pool), 'cache_cmp_v': nrm(pool),
        'cache_sel_k': nrm(pool), 'cache_sel_v': nrm(pool),
        'cache_win_k': nrm(win), 'cache_win_v': nrm(win),
        'state_mlstm_c': nrm((DEPTH, DEC_BATCH, MLSTM_HEADS, MLSTM_HEAD_DIM, MLSTM_HEAD_DIM), 0.3),
        'state_mlstm_n': nrm((DEPTH, DEC_BATCH, MLSTM_HEADS, MLSTM_HEAD_DIM), 0.3),
        'state_mlstm_m': nrm((DEPTH, DEC_BATCH, MLSTM_HEADS), 0.5),
        'state_conv': nrm((DEPTH, DEC_BATCH, CONV_WIDTH - 1, MLSTM_WIDTH)),
        'cache_mem_k': nrm(memc), 'cache_mem_v': nrm(memc),
        'page_table': page_table,
        'mem_prompt': nrm((BATCH, N_MEM, d)),
        'g_mix': gain((DEPTH, d)),
        'w_in': nrm((DEPTH, d, D_IN), d ** -0.5),
        'b_in': b_in,
        'ck_pe': nrm((DEPTH, CMP_BLOCK, NSA_HEAD_DIM), 0.1),
        'ck_w1': nrm((DEPTH, CMP_BLOCK, NSA_HEAD_DIM, CMP_HIDDEN), (CMP_BLOCK * NSA_HEAD_DIM) ** -0.5),
        'ck_b1': nrm((DEPTH, CMP_HIDDEN), 0.01),
        'ck_w2': nrm((DEPTH, CMP_HIDDEN, NSA_HEAD_DIM), CMP_HIDDEN ** -0.5),
        'cv_pe': nrm((DEPTH, CMP_BLOCK, NSA_HEAD_DIM), 0.1),
        'cv_w1': nrm((DEPTH, CMP_BLOCK, NSA_HEAD_DIM, CMP_HIDDEN), (CMP_BLOCK * NSA_HEAD_DIM) ** -0.5),
        'cv_b1': nrm((DEPTH, CMP_HIDDEN), 0.01),
        'cv_w2': nrm((DEPTH, CMP_HIDDEN, NSA_HEAD_DIM), CMP_HIDDEN ** -0.5),
        'conv_w': nrm((DEPTH, CONV_WIDTH, MLSTM_WIDTH), CONV_WIDTH ** -0.5),
        'conv_b': nrm((DEPTH, MLSTM_WIDTH), 0.01),
        'm_wq': nrm((DEPTH, MLSTM_HEADS, MLSTM_HEAD_DIM, MLSTM_HEAD_DIM), MLSTM_HEAD_DIM ** -0.5),
        'm_wk': nrm((DEPTH, MLSTM_HEADS, MLSTM_HEAD_DIM, MLSTM_HEAD_DIM), MLSTM_HEAD_DIM ** -0.5),
        'm_norm_g': gain((DEPTH, MLSTM_WIDTH)),
        'w_out': nrm((DEPTH, MIX_WIDTH, d), MIX_WIDTH ** -0.5),
        'g_cross': gain((DEPTH, d)),
        'g_mem': gain((DEPTH, d)),
        'w_cq': nrm((DEPTH, d, cw), d ** -0.5),
        'w_ck': nrm((DEPTH, d, cw), d ** -0.5),
        'w_cv': nrm((DEPTH, d, cw), d ** -0.5),
        'w_co': nrm((DEPTH, cw, d), cw ** -0.5),
        'g_ffn': gain((DEPTH, d)),
        'w_group': nrm((DEPTH, d, N_GROUPS), d ** -0.5),
        'b_group': nrm((DEPTH, N_GROUPS), 0.01),
        'w_expert': nrm((DEPTH, d, N_GROUPS * EXPERTS_PER_GROUP), d ** -0.5),
        'b_expert': nrm((DEPTH, N_GROUPS * EXPERTS_PER_GROUP), 0.01),
        'e_w1': nrm(ge + (d, D_EXPERT), d ** -0.5),
        'e_w3': nrm(ge + (d, D_EXPERT), d ** -0.5),
        'e_w2': nrm(ge + (D_EXPERT, d), D_EXPERT ** -0.5),
        'g_final': gain((d,)),
    }


def reference(x_prompt, x_sample, cache_cmp_k, cache_cmp_v, cache_sel_k, cache_sel_v, cache_win_k, cache_win_v,
              state_mlstm_c, state_mlstm_n, state_mlstm_m, state_conv, cache_mem_k, cache_mem_v, page_table, mem_prompt,
              g_mix, w_in, b_in, ck_pe, ck_w1, ck_b1, ck_w2, cv_pe, cv_w1, cv_b1, cv_w2, conv_w, conv_b, m_wq, m_wk,
              m_norm_g, w_out, g_cross, g_mem, w_cq, w_ck, w_cv, w_co, g_ffn, w_group, b_group, w_expert, b_expert,
              e_w1, e_w3, e_w2, g_final):
    xp, xs = x_prompt, x_sample
    prompt_states, sample_states = [], []
    for l in range(DEPTH):
        lw = dict(g_mix=g_mix[l], w_in=w_in[l], b_in=b_in[l], ck_pe=ck_pe[l], ck_w1=ck_w1[l], ck_b1=ck_b1[l],
                  ck_w2=ck_w2[l], cv_pe=cv_pe[l], cv_w1=cv_w1[l], cv_b1=cv_b1[l], cv_w2=cv_w2[l], conv_w=conv_w[l],
                  conv_b=conv_b[l], m_wq=m_wq[l], m_wk=m_wk[l], m_norm_g=m_norm_g[l], w_out=w_out[l],
                  g_cross=g_cross[l], g_mem=g_mem[l], w_cq=w_cq[l], w_ck=w_ck[l], w_cv=w_cv[l], w_co=w_co[l],
                  g_ffn=g_ffn[l], w_group=w_group[l], b_group=b_group[l], w_expert=w_expert[l],
                  b_expert=b_expert[l], e_w1=e_w1[l], e_w3=e_w3[l], e_w2=e_w2[l])
        xp, st_p = prompt_layer(xp, mem_prompt, lw)
        xs, st_s = sample_layer(xs, cache_cmp_k[l], cache_cmp_v[l], cache_sel_k[l], cache_sel_v[l],
                                cache_win_k[l], cache_win_v[l], state_mlstm_c[l], state_mlstm_n[l],
                                state_mlstm_m[l], state_conv[l], cache_mem_k[l], cache_mem_v[l], page_table, lw)
        prompt_states.append(st_p)
        sample_states.append(st_s)
    y_prompt = rmsnorm(xp, g_final)
    y_sample = rmsnorm(xs, g_final)
    (p_cmp_k, p_cmp_v, p_sel_k, p_sel_v, p_win_k, p_win_v, p_c, p_n, p_m, p_conv, p_mem_k, p_mem_v) = [
        jnp.stack(a) for a in zip(*prompt_states)]
    (s_cmp_k, s_cmp_v, s_sel_k, s_sel_v, s_win_k, s_win_v, s_c, s_n, s_m, s_conv) = [
        jnp.stack(a) for a in zip(*sample_states)]
    return (y_prompt, y_sample,
            p_cmp_k, p_cmp_v, p_sel_k, p_sel_v, p_win_k, p_win_v, p_c, p_n, p_m, p_conv, p_mem_k, p_mem_v,
            s_cmp_k, s_cmp_v, s_sel_k, s_sel_v, s_win_k, s_win_v, s_c, s_n, s_m, s_conv)
```

```python
import functools
import math

import jax
import jax.numpy as jnp
from jax import lax
from jax.experimental import pallas as pl
from jax.experimental.pallas import tpu as pltpu

F32 = jnp.float32
BF16 = jnp.bfloat16
I32 = jnp.int32

D_MODEL = 1024
PAGE = 128
DH = 64
NSA_HEADS = 8
KVH = 2
GQA = 4
CMP_STRIDE = 16
CMP_BLOCK = 32
CMP_HIDDEN = 128
SEL_BLOCK = 64
N_SEL = 16
WINDOW = 512
FORCE_BONUS = 1.0e4
Q_BLOCK = 128
MH = 4
MD = 128
MW = MH * MD
CONV_W = 4
N_MEM = 256
CH = 4
CD = 128
N_EXPERT = 32
EPG = 8
N_GROUPS = 4
D_EXPERT = 256
EPS = 1e-6
LANES = 128
SEL_TILE = 512
VMEM_LIMIT = 56 * 1024 * 1024

NEG_INF = float("-inf")


def _iota(shape, dim):
    return lax.broadcasted_iota(I32, shape, dim)


def _dot(a, b):
    return jnp.dot(a, b, preferred_element_type=F32)


def _dot_nt(a, b):
    return lax.dot_general(a, b, (((1,), (1,)), ((), ())), preferred_element_type=F32)


def _rms(x, g):
    return x * lax.rsqrt(jnp.mean(x * x, axis=-1, keepdims=True) + EPS) * g


def _sigmoid(x):
    return 1.0 / (1.0 + jnp.exp(-x))


def _silu(x):
    return x * _sigmoid(x)


def _log_sigmoid(x):
    return jnp.minimum(x, 0.0) - jnp.log(1.0 + jnp.exp(-jnp.abs(x)))


def _gelu_tanh(x):
    c = math.sqrt(2.0 / math.pi)
    return 0.5 * x * (1.0 + jnp.tanh(c * (x + 0.044715 * (x * x * x))))


def _masked_softmax(s, mask):
    s = jnp.where(mask, s, NEG_INF)
    m = jnp.max(s, axis=-1, keepdims=True)
    m = jnp.where(m == NEG_INF, 0.0, m)
    e = jnp.where(mask, jnp.exp(s - m), 0.0)
    den = jnp.sum(e, axis=-1, keepdims=True)
    return e / jnp.where(den > 0, den, 1.0)


def _split3_dot(x, w):
    hi = x.astype(BF16)
    r1 = x - hi.astype(F32)
    mid = r1.astype(BF16)
    lo = (r1 - mid.astype(F32)).astype(BF16)
    return _dot(hi, w) + _dot(mid, w) + _dot(lo, w)


def _slope_col(row_g, h):
    sl = [2.0 ** (-(GQA * h + g + 1.0)) for g in range(GQA)]
    return jnp.where(row_g == 0, sl[0], jnp.where(row_g == 1, sl[1], jnp.where(row_g == 2, sl[2], sl[3])))


def _topk_rounds(score, blk_f):
    sel = jnp.zeros(score.shape, F32)
    cols = []
    sc = score
    for _ in range(N_SEL):
        m = jnp.max(sc, axis=1, keepdims=True)
        idx = jnp.min(jnp.where(sc == m, blk_f, 1.0e9), axis=1, keepdims=True)
        pick = blk_f == idx
        sel = jnp.where(pick, 1.0, sel)
        sc = jnp.where(pick, NEG_INF, sc)
        cols.append(idx)
    return sel, cols


def _sel_scores(imp_slc, qpos_col, n_blocks):
    blk = _iota(imp_slc.shape, 1)
    cur = qpos_col >> 6
    valid = (blk * SEL_BLOCK) <= qpos_col
    forced = (blk == 0) | (blk == cur) | (blk == cur - 1)
    score = jnp.where(valid, imp_slc + jnp.where(forced, FORCE_BONUS, 0.0), -1.0)
    return jnp.where(blk < n_blocks, score, NEG_INF)


def _inproj_kernel(x_ref, g_ref, wq_ref, bq_ref, wkv_ref, bkv_ref, wg_ref, bg_ref, wm_ref, bm_ref,
                   wif_ref, bif_ref, wift_ref, bift_ref,
                   q_out, kvf_out, kvb_out, gate_out, m3_out, gif_out, gift_out):
    h = _rms(x_ref[...], g_ref[...]).astype(BF16)
    q_out[...] = ((_dot(h, wq_ref[...]) + bq_ref[...]) * (DH ** -0.5)).astype(BF16)
    kv = _dot(h, wkv_ref[...]) + bkv_ref[...]
    kvf_out[...] = kv
    kvb_out[...] = kv.astype(BF16)
    gate_out[...] = _sigmoid(_dot(h, wg_ref[...]) + bg_ref[...])
    m3_out[...] = _dot(h, wm_ref[...]) + bm_ref[...]
    gi = _dot(h, wif_ref[...]) + bif_ref[...]
    lane = _iota(gi.shape, 1)
    gif_out[...] = jnp.where((lane >= MH) & (lane < 2 * MH), _log_sigmoid(gi), gi)
    git = (_dot_nt(wift_ref[...], h) + bift_ref[...])[0:8]
    row = _iota(git.shape, 0)
    gift_out[...] = jnp.where(row >= MH, _log_sigmoid(git), git)


def _in_projection(x, wts, tm):
    m = x.shape[0]
    full = lambda a: pl.BlockSpec(a.shape, lambda i: (0,) * a.ndim)
    rows = lambda n: pl.BlockSpec((tm, n), lambda i: (i, 0))
    names = ["g_mix", "wq", "bq", "wkv", "bkv", "wg", "bg", "wm", "bm", "wif", "bif", "wift", "bift"]
    ws = [wts[n] for n in names]
    return pl.pallas_call(
        _inproj_kernel,
        grid=(m // tm,),
        in_specs=[rows(D_MODEL)] + [full(a) for a in ws],
        out_specs=[rows(1024), rows(768), rows(768), rows(LANES), rows(3 * MW), rows(LANES),
                   pl.BlockSpec((8, tm), lambda i: (0, i))],
        out_shape=[jax.ShapeDtypeStruct((m, 1024), BF16), jax.ShapeDtypeStruct((m, 768), F32),
                   jax.ShapeDtypeStruct((m, 768), BF16), jax.ShapeDtypeStruct((m, LANES), F32),
                   jax.ShapeDtypeStruct((m, 3 * MW), F32), jax.ShapeDtypeStruct((m, LANES), F32),
                   jax.ShapeDtypeStruct((8, m), F32)],
        compiler_params=pltpu.CompilerParams(dimension_semantics=("arbitrary",), vmem_limit_bytes=VMEM_LIMIT),
        name="in_projection",
    )(x, *ws)


def _compress_core(x_ref, wab_ref, bias, w2_ref, nchunk):
    acc = None
    for pp in range(CMP_STRIDE // 2):
        xa = x_ref[pl.ds(2 * pp, nchunk, stride=CMP_STRIDE), :].astype(BF16)
        xb = x_ref[pl.ds(2 * pp + 1, nchunk, stride=CMP_STRIDE), :].astype(BF16)
        d = _dot(jnp.concatenate([xa, xb], axis=1), wab_ref[pp])
        acc = d if acc is None else acc + d
    hw = KVH * CMP_HIDDEN
    z = acc[:, :hw] + pltpu.roll(acc[:, hw:], nchunk - 1, 0) + bias
    return _dot(_gelu_tanh(z).astype(BF16), w2_ref[...])


def _compress_kernel(x_ref, wab_ref, bias_ref, w2_ref, out_ref, *, nchunk):
    out_ref[0] = _compress_core(x_ref, wab_ref.at[0], bias_ref[0], w2_ref.at[0], nchunk)


def _compress_prompt(kvf, cw):
    t = kvf.shape[0]
    nchunk = t // CMP_STRIDE
    return pl.pallas_call(
        functools.partial(_compress_kernel, nchunk=nchunk),
        grid=(2,),
        in_specs=[pl.BlockSpec((t, LANES), lambda i: (0, i)),
                  pl.BlockSpec((1,) + cw["wab"].shape[1:], lambda i: (i, 0, 0, 0)),
                  pl.BlockSpec((1, 1, 256), lambda i: (i, 0, 0)),
                  pl.BlockSpec((1, 256, LANES), lambda i: (i, 0, 0))],
        out_specs=pl.BlockSpec((1, nchunk, LANES), lambda i: (i, 0, 0)),
        out_shape=jax.ShapeDtypeStruct((2, nchunk, LANES), F32),
        compiler_params=pltpu.CompilerParams(dimension_semantics=("arbitrary",), vmem_limit_bytes=VMEM_LIMIT),
        name="compress_prompt",
    )(kvf, cw["wab"], cw["bias"], cw["w2"])


def _nsa_prompt_kernel(q_ref, gate_ref, kc_ref, vc_ref, ks_ref, vs_ref, kw_ref, vw_ref, msel_ref, out_ref,
                       sel_ref, m_ref, l_ref, acc_ref, *, t_len):
    i = pl.program_id(0)
    qs = i * Q_BLOCK
    nb = t_len // SEL_BLOCK
    nbp = msel_ref.shape[1]
    nc = t_len // CMP_STRIDE
    rows = GQA * Q_BLOCK
    row = _iota((rows, 1), 0)
    qpos = qs + (row & (Q_BLOCK - 1))
    row_g = row >> 7
    qpos_q = qs + _iota((Q_BLOCK, 1), 0)

    qh, slope, o_cmp, scores = [], [], [], []
    for h in range(KVH):
        q = jnp.concatenate([q_ref[:, (GQA * h + g) * LANES:(GQA * h + g + 1) * LANES] for g in range(GQA)], axis=0)
        qh.append(q)
        slope.append(_slope_col(row_g, h))
        cidx = _iota((1, nc), 1)
        dist = qpos - (cidx * CMP_STRIDE + (CMP_BLOCK - 1))
        s = _dot_nt(q, kc_ref[...]) - slope[h] * dist.astype(F32)
        p = _masked_softmax(s, (dist >= 0) & (cidx < nc - 1))
        o_cmp.append(_dot(p.astype(BF16), vc_ref[...]))
        imp = p[0:Q_BLOCK] + p[Q_BLOCK:2 * Q_BLOCK] + p[2 * Q_BLOCK:3 * Q_BLOCK] + p[3 * Q_BLOCK:]
        scores.append(_sel_scores(_split3_dot(imp, msel_ref[...]), qpos_q, nb))

    blk_f = _iota((KVH * Q_BLOCK, nbp), 1).astype(F32)
    sel, _ = _topk_rounds(jnp.concatenate(scores, axis=0), blk_f)
    sel_ref[...] = sel.astype(BF16)

    grp = jnp.where((_iota((nbp, LANES), 0) >> 3) == _iota((nbp, LANES), 1), 1.0, 0.0).astype(BF16)
    lane = _iota((16, LANES), 1)
    n_causal = i // (SEL_TILE // Q_BLOCK) + 1
    dsel = _iota((nbp, SEL_TILE), 0) - (_iota((nbp, SEL_TILE), 1) >> 6)

    o_sel, o_win = [], []
    for h in range(KVH):
        any_blk = jnp.max(sel[h * Q_BLOCK:(h + 1) * Q_BLOCK], axis=0, keepdims=True)
        cnt = _dot(jnp.broadcast_to(any_blk, (16, nbp)).astype(BF16), grp)
        hit = (cnt > 0.5) & (_iota((16, LANES), 0) == 0)
        lo = jnp.sum(jnp.where(hit & (lane < 16), (1 << (lane & 15)).astype(F32), 0.0)).astype(I32)
        hi = jnp.sum(jnp.where(hit & (lane >= 16) & (lane < 32), (1 << (lane & 15)).astype(F32), 0.0)).astype(I32)

        m_ref[...] = jnp.full(m_ref.shape, NEG_INF, F32)
        l_ref[...] = jnp.zeros(l_ref.shape, F32)
        acc_ref[...] = jnp.zeros(acc_ref.shape, F32)

        def tile(t, carry, h=h, lo=lo, hi=hi):
            word = jnp.where(t < 16, lo, hi)
            active = ((word >> (t & 15)) & 1) == 1

            @pl.when(active)
            def _():
                k0 = pl.multiple_of(t * SEL_TILE, SEL_TILE)
                kt = ks_ref[pl.ds(k0, SEL_TILE), :]
                vt = vs_ref[pl.ds(k0, SEL_TILE), :]
                dist = qpos - (k0 + _iota((1, SEL_TILE), 1))
                expand = jnp.where(dsel == t * (SEL_TILE // SEL_BLOCK), 1.0, 0.0).astype(BF16)
                mk = _dot(sel_ref[h * Q_BLOCK:(h + 1) * Q_BLOCK, :], expand)
                mk = jnp.concatenate([mk] * GQA, axis=0)
                mask = (mk > 0.5) & (dist >= 0)
                s = jnp.where(mask, _dot_nt(qh[h], kt) - slope[h] * dist.astype(F32), NEG_INF)
                m_old = m_ref[...]
                m_new = jnp.maximum(m_old, jnp.max(s, axis=1, keepdims=True))
                m_safe = jnp.where(m_new == NEG_INF, 0.0, m_new)
                alpha = jnp.exp(m_old - m_safe)
                p = jnp.exp(s - m_safe)
                l_ref[...] = alpha * l_ref[...] + jnp.sum(p, axis=1, keepdims=True)
                acc_ref[...] = alpha * acc_ref[...] + _dot(p.astype(BF16), vt)
                m_ref[...] = m_new

            return carry

        lax.fori_loop(0, n_causal, tile, 0)
        l = l_ref[...]
        o_sel.append(acc_ref[...] / jnp.where(l > 0, l, 1.0))

        wlen = WINDOW + Q_BLOCK
        w0 = pl.multiple_of(jnp.maximum(qs - WINDOW, 0), Q_BLOCK)
        dist = qpos - (w0 + _iota((1, wlen), 1))
        s = _dot_nt(qh[h], kw_ref[pl.ds(w0, wlen), :]) - slope[h] * dist.astype(F32)
        p = _masked_softmax(s, (dist >= 0) & (dist < WINDOW))
        o_win.append(_dot(p.astype(BF16), vw_ref[pl.ds(w0, wlen), :]))

    gates = gate_ref[...]
    lane_q = _iota((Q_BLOCK, LANES), 1)
    for g in range(GQA):
        parts = []
        sl = slice(g * Q_BLOCK, (g + 1) * Q_BLOCK)
        for h in range(KVH):
            c = 3 * (GQA * h + g)
            parts.append(gates[:, c:c + 1] * o_cmp[h][sl] + gates[:, c + 1:c + 2] * o_sel[h][sl]
                         + gates[:, c + 2:c + 3] * o_win[h][sl])
        out_ref[:, g * LANES:(g + 1) * LANES] = jnp.where(lane_q < DH, parts[0], parts[1]).astype(BF16)


def _nsa_prompt(q_pad, gates, kcmp, vcmp, kvb, msel):
    t = q_pad.shape[0]
    nbp = msel.shape[1]
    col = lambda c: pl.BlockSpec((t, LANES), lambda i, c=c: (0, c))
    full = lambda a: pl.BlockSpec(a.shape, lambda i: (0,) * a.ndim)
    return pl.pallas_call(
        functools.partial(_nsa_prompt_kernel, t_len=t),
        grid=(t // Q_BLOCK,),
        in_specs=[pl.BlockSpec((Q_BLOCK, 1024), lambda i: (i, 0)), pl.BlockSpec((Q_BLOCK, LANES), lambda i: (i, 0)),
                  full(kcmp), full(vcmp), col(2), col(3), col(4), col(5), full(msel)],
        out_specs=pl.BlockSpec((Q_BLOCK, 512), lambda i: (i, 0)),
        out_shape=jax.ShapeDtypeStruct((t, 512), BF16),
        scratch_shapes=[pltpu.VMEM((KVH * Q_BLOCK, nbp), BF16), pltpu.VMEM((GQA * Q_BLOCK, 1), F32),
                        pltpu.VMEM((GQA * Q_BLOCK, 1), F32), pltpu.VMEM((GQA * Q_BLOCK, LANES), F32)],
        compiler_params=pltpu.CompilerParams(dimension_semantics=("arbitrary",), vmem_limit_bytes=VMEM_LIMIT),
        name="nsa_prompt",
    )(q_pad, gates, kcmp, vcmp, kvb, kvb, kvb, kvb, msel)


def _nsa_dec1_kernel(pt_ref, ckp_ref, cvp_ref, q_ref, wab_ref, bias_ref, w2_ref, msel_ref,
                     ocmp_ref, idx_ref, kbuf, vbuf, sem, *, n_pages, n_tok):
    b = pl.program_id(0)
    past = n_pages * PAGE
    nc = past // CMP_STRIDE
    ns = past // SEL_BLOCK + 1
    nsp = msel_ref.shape[1]

    def copies(j):
        pg = pt_ref[b * n_pages + j]
        dst = pl.ds(pl.multiple_of(j * PAGE, PAGE), PAGE)
        return (pltpu.make_async_copy(ckp_ref.at[pg], kbuf.at[dst], sem.at[0]),
                pltpu.make_async_copy(cvp_ref.at[pg], vbuf.at[dst], sem.at[1]))

    def start(j, c):
        ck, cv = copies(j)
        ck.start()
        cv.start()
        return c

    def wait(j, c):
        ck, cv = copies(j)
        ck.wait()
        cv.wait()
        return c

    lax.fori_loop(0, n_pages, start, 0)
    lax.fori_loop(0, n_pages, wait, 0)

    kc = _compress_core(kbuf, wab_ref.at[0], bias_ref[0], w2_ref.at[0], nc).astype(BF16)
    vc = _compress_core(vbuf, wab_ref.at[1], bias_ref[1], w2_ref.at[1], nc).astype(BF16)

    rows = 8 * n_tok
    row = _iota((rows, 1), 0)
    row_g = row & 7
    qpos = past + (row >> 3)
    cidx = _iota((1, nc), 1)
    dist = qpos - (cidx * CMP_STRIDE + (CMP_BLOCK - 1))
    cmask = (dist >= 0) & (cidx < nc - 1) & (row_g < GQA)
    qpos_t = past + _iota((8, 1), 0)
    scores = []
    for h in range(KVH):
        s = _dot_nt(q_ref[0, h], kc) - _slope_col(row_g, h) * dist.astype(F32)
        p = _masked_softmax(s, cmask)
        ocmp_ref[0, h] = _dot(p.astype(BF16), vc)
        imp = jnp.sum(p.reshape(n_tok, 8, nc), axis=1)
        imp = jnp.concatenate([imp, jnp.zeros((8 - n_tok, nc), F32)], axis=0) if n_tok < 8 else imp
        scores.append(_sel_scores(_split3_dot(imp, msel_ref[...]), qpos_t, ns))
    score = jnp.concatenate(scores, axis=0)
    _, cols = _topk_rounds(score, _iota(score.shape, 1).astype(F32))
    lane = _iota((KVH * 8, LANES), 1)
    idx = jnp.zeros((KVH * 8, LANES), F32)
    for k, c in enumerate(cols):
        idx = jnp.where(lane == k, c, idx)
    idx_ref[0] = idx.astype(I32)


def _nsa_dec1(page_table, ck_pool, cv_pool, q_dec, cw, msel):
    bsz, n_pages = page_table.shape
    n_tok = q_dec.shape[2] // 8
    past = n_pages * PAGE
    full = lambda a: pl.BlockSpec(a.shape, lambda b, pt: (0,) * a.ndim)
    grid_spec = pltpu.PrefetchScalarGridSpec(
        num_scalar_prefetch=1,
        grid=(bsz,),
        in_specs=[pl.BlockSpec(memory_space=pl.ANY), pl.BlockSpec(memory_space=pl.ANY),
                  pl.BlockSpec((1,) + q_dec.shape[1:], lambda b, pt: (b, 0, 0, 0)),
                  full(cw["wab"]), full(cw["bias"]), full(cw["w2"]), full(msel)],
        out_specs=[pl.BlockSpec((1, KVH, 8 * n_tok, LANES), lambda b, pt: (b, 0, 0, 0)),
                   pl.BlockSpec((1, KVH * 8, LANES), lambda b, pt: (b, 0, 0))],
        scratch_shapes=[pltpu.VMEM((past, LANES), F32), pltpu.VMEM((past, LANES), F32),
                        pltpu.SemaphoreType.DMA((2,))],
    )
    return pl.pallas_call(
        functools.partial(_nsa_dec1_kernel, n_pages=n_pages, n_tok=n_tok),
        grid_spec=grid_spec,
        out_shape=[jax.ShapeDtypeStruct((bsz, KVH, 8 * n_tok, LANES), F32),
                   jax.ShapeDtypeStruct((bsz, KVH * 8, LANES), I32)],
        compiler_params=pltpu.CompilerParams(dimension_semantics=("arbitrary",), vmem_limit_bytes=VMEM_LIMIT),
        name="nsa_decode_cmp",
    )(page_table.reshape(-1), ck_pool, cv_pool, q_dec, cw["wab"], cw["bias"], cw["w2"], msel)


def _nsa_dec2_kernel(pt_ref, idx_ref, skp_ref, svp_ref, q_ref, gate_ref, ocmp_ref, ksn_ref, vsn_ref,
                     wk_ref, wv_ref, kwn_ref, vwn_ref, out_ref, kg, vg, sem, *, n_pages, n_tok):
    b = pl.program_id(0)
    past = n_pages * PAGE
    ns = past // SEL_BLOCK + 1
    n_slot = KVH * n_tok * N_SEL
    glen = N_SEL * SEL_BLOCK

    def copies(e):
        j = jnp.minimum(idx_ref[b * n_slot + e], ns - 2)
        pg = pt_ref[b * n_pages + (j >> 1)]
        src = pl.ds(pl.multiple_of((j & 1) * SEL_BLOCK, SEL_BLOCK), SEL_BLOCK)
        dst = pl.ds(pl.multiple_of(e * SEL_BLOCK, SEL_BLOCK), SEL_BLOCK)
        return (pltpu.make_async_copy(skp_ref.at[pg, src], kg.at[dst], sem.at[0]),
                pltpu.make_async_copy(svp_ref.at[pg, src], vg.at[dst], sem.at[1]))

    def start(e, c):
        ck, cv = copies(e)
        ck.start()
        cv.start()
        return c

    def wait(e, c):
        ck, cv = copies(e)
        ck.wait()
        cv.wait()
        return c

    lax.fori_loop(0, n_slot, start, 0)

    rows = 8 * n_tok
    row = _iota((rows, 1), 0)
    row_g = row & 7
    qpos = past + (row >> 3)
    o_win = []
    wb = wk_ref.shape[1]
    dist_o = qpos - (past - wb + _iota((1, wb), 1))
    dist_n = qpos - (past + _iota((1, 8), 1))
    for h in range(KVH):
        sl = _slope_col(row_g, h)
        s_o = jnp.where((dist_o >= 0) & (dist_o < WINDOW),
                        _dot_nt(q_ref[0, h], wk_ref[0].astype(BF16)) - sl * dist_o.astype(F32), NEG_INF)
        s_n = jnp.where((dist_n >= 0) & (dist_n < WINDOW),
                        _dot_nt(q_ref[0, h], kwn_ref[0].astype(BF16)) - sl * dist_n.astype(F32), NEG_INF)
        m = jnp.maximum(jnp.max(s_o, axis=1, keepdims=True), jnp.max(s_n, axis=1, keepdims=True))
        m = jnp.where(m == NEG_INF, 0.0, m)
        p_o = jnp.exp(s_o - m)
        p_n = jnp.exp(s_n - m)
        den = jnp.sum(p_o, axis=1, keepdims=True) + jnp.sum(p_n, axis=1, keepdims=True)
        num = _dot(p_o.astype(BF16), wv_ref[0].astype(BF16)) + _dot(p_n.astype(BF16), vwn_ref[0].astype(BF16))
        o_win.append(num / jnp.where(den > 0, den, 1.0))

    lax.fori_loop(0, n_slot, wait, 0)

    lane_g = _iota((1, glen), 1)
    ksn = ksn_ref[0].astype(BF16)
    vsn = vsn_ref[0].astype(BF16)
    for h in range(KVH):
        o_rows = []
        for t in range(n_tok):
            base = (h * n_tok + t) * N_SEL
            kpos = jnp.full((1, glen), past + 2 * SEL_BLOCK, I32)
            has_new = jnp.zeros((), I32)
            for k in range(N_SEL):
                j = idx_ref[b * n_slot + base + k]
                is_new = j == ns - 1
                has_new = has_new | is_new.astype(I32)
                start_pos = jnp.where(is_new, past + 2 * SEL_BLOCK, j * SEL_BLOCK)
                kpos = jnp.where((lane_g >> 6) == k, start_pos + (lane_g & (SEL_BLOCK - 1)), kpos)
            q = q_ref[0, h, 8 * t:8 * t + 8, :]
            qp = past + t
            g8 = _iota((8, 1), 0)
            sl = _slope_col(g8, h)
            dist_s = qp - kpos
            off = pl.multiple_of(base * SEL_BLOCK, SEL_BLOCK)
            s_s = jnp.where(dist_s >= 0,
                            _dot_nt(q, kg[pl.ds(off, glen), :].astype(BF16)) - sl * dist_s.astype(F32), NEG_INF)
            dist_n = qp - (past + _iota((1, 8), 1))
            s_n = jnp.where((dist_n >= 0) & (has_new > 0), _dot_nt(q, ksn) - sl * dist_n.astype(F32), NEG_INF)
            m = jnp.maximum(jnp.max(s_s, axis=1, keepdims=True), jnp.max(s_n, axis=1, keepdims=True))
            m = jnp.where(m == NEG_INF, 0.0, m)
            p_s = jnp.exp(s_s - m)
            p_n = jnp.exp(s_n - m)
            den = jnp.sum(p_s, axis=1, keepdims=True) + jnp.sum(p_n, axis=1, keepdims=True)
            num = _dot(p_s.astype(BF16), vg[pl.ds(off, glen), :].astype(BF16)) + _dot(p_n.astype(BF16), vsn)
            o_rows.append(num / jnp.where(den > 0, den, 1.0))
        o_sel = jnp.concatenate(o_rows, axis=0)
        gt = gate_ref[0, h]
        out_ref[0, h] = gt[:, 0:1] * ocmp_ref[0, h] + gt[:, 1:2] * o_sel + gt[:, 2:3] * o_win[h]


def _nsa_dec2(page_table, idx, sk_pool, sv_pool, q_dec, g_dec, o_cmp, ks_new, vs_new, win_k, win_v, kw_new, vw_new):
    bsz, n_pages = page_table.shape
    n_tok = q_dec.shape[2] // 8
    blk4 = lambda a: pl.BlockSpec((1,) + a.shape[1:], lambda b, pt, ix: (b, 0, 0, 0))
    blk3 = lambda a: pl.BlockSpec((1,) + a.shape[1:], lambda b, pt, ix: (b, 0, 0))
    n_rows = KVH * n_tok * N_SEL * SEL_BLOCK
    grid_spec = pltpu.PrefetchScalarGridSpec(
        num_scalar_prefetch=2,
        grid=(bsz,),
        in_specs=[pl.BlockSpec(memory_space=pl.ANY), pl.BlockSpec(memory_space=pl.ANY),
                  blk4(q_dec), blk4(g_dec), blk4(o_cmp), blk3(ks_new), blk3(vs_new),
                  blk3(win_k), blk3(win_v), blk3(kw_new), blk3(vw_new)],
        out_specs=pl.BlockSpec((1, KVH, 8 * n_tok, LANES), lambda b, pt, ix: (b, 0, 0, 0)),
        scratch_shapes=[pltpu.VMEM((n_rows, LANES), F32), pltpu.VMEM((n_rows, LANES), F32),
                        pltpu.SemaphoreType.DMA((2,))],
    )
    return pl.pallas_call(
        functools.partial(_nsa_dec2_kernel, n_pages=n_pages, n_tok=n_tok),
        grid_spec=grid_spec,
        out_shape=jax.ShapeDtypeStruct((bsz, KVH, 8 * n_tok, LANES), F32),
        compiler_params=pltpu.CompilerParams(dimension_semantics=("arbitrary",), vmem_limit_bytes=VMEM_LIMIT),
        name="nsa_decode_sel",
    )(page_table.reshape(-1), idx, sk_pool, sv_pool, q_dec, g_dec, o_cmp, ks_new, vs_new, win_k, win_v,
      kw_new, vw_new)


def _mlstm_kernel(m3_ref, gif_ref, gift_ref, conv0_ref, c0_ref, m0_ref, cw_ref, cb_ref, wq_ref, wk_ref, wkt_ref,
                  ng_ref, out_ref, c1_ref, m1_ref, cbuf, cst, mst, *, lc):
    c = pl.program_id(1)

    @pl.when(c == 0)
    def _():
        cbuf[0:8, :] = conv0_ref[0]
        cst[...] = c0_ref[0]
        mst[...] = m0_ref[0]

    cbuf[8:8 + lc, :] = m3_ref[0, :, 0:MW]
    conv = cb_ref[...] + cbuf[5:5 + lc, :] * cw_ref[0:1, :]
    for j in range(1, CONV_W):
        conv = conv + cbuf[5 + j:5 + j + lc, :] * cw_ref[j:j + 1, :]
    cbuf[0:8, :] = cbuf[lc:lc + 8, :]
    csil = _silu(conv).astype(BF16)

    tri = _iota((lc, lc), 1) <= _iota((lc, lc), 0)
    tri_t = _iota((lc, lc), 0) <= _iota((lc, lc), 1)
    gif = gif_ref[0]
    gift = gift_ref[0]
    ones_col = jnp.where(_iota((lc, LANES), 1) == 0, 1.0, 0.0)
    m_all = mst[...]
    for h in range(MH):
        hs = slice(h * MD, (h + 1) * MD)
        ch = csil[:, hs]
        q = _dot(ch, wq_ref[h]).astype(BF16)
        k = (_dot(ch, wk_ref[h]) * (MD ** -0.5)).astype(BF16)
        kt = _dot_nt(wkt_ref[h], ch) * (MD ** -0.5)
        vaug = jnp.concatenate([m3_ref[0, :, MW + h * MD:MW + (h + 1) * MD], ones_col], axis=1).astype(BF16)
        lf_col = gif[:, MH + h:MH + h + 1]
        ig_row = gift[h:h + 1, :]
        lf_row = gift[MH + h:MH + h + 1, :]
        bcum_col = jnp.sum(jnp.where(tri, lf_row, 0.0), axis=1, keepdims=True)
        bcum_row = jnp.sum(jnp.where(tri_t, lf_col, 0.0), axis=0, keepdims=True)
        m_prev = m_all[0:1, h:h + 1]
        dlog = jnp.where(tri, bcum_col - bcum_row + ig_row, NEG_INF)
        inter = bcum_col + m_prev
        m_t = jnp.maximum(inter, jnp.max(dlog, axis=1, keepdims=True))
        w_intra = jnp.exp(dlog - m_t)
        w_inter = jnp.exp(inter - m_t)
        sw = (_dot_nt(q, k) * w_intra).astype(BF16)
        cprev = cst[h]
        numden = w_inter * _dot(q, cprev.astype(BF16)) + _dot(sw, vaug)
        den = numden[:, MD:MD + 1]
        hh = numden[:, 0:MD] / jnp.maximum(jnp.abs(den), jnp.exp(-m_t))
        b_last = bcum_col[lc - 1:lc, :]
        dl_end = b_last - bcum_row + ig_row
        m_new = jnp.maximum(b_last + m_prev, jnp.max(dl_end, axis=1, keepdims=True))
        a_prev = jnp.exp(b_last + m_prev - m_new)
        w_end = jnp.exp(dl_end - m_new)
        cst[h] = a_prev * cprev + _dot((kt * w_end).astype(BF16), vaug)
        m_all = jnp.where(_iota(m_all.shape, 1) == h, m_new, m_all)
        hn = hh * lax.rsqrt(jnp.mean(hh * hh, axis=-1, keepdims=True) + EPS) * ng_ref[:, hs]
        out_ref[0, :, hs] = (_sigmoid(m3_ref[0, :, 2 * MW + h * MD:2 * MW + (h + 1) * MD]) * hn).astype(BF16)
    mst[...] = m_all

    @pl.when(c == pl.num_programs(1) - 1)
    def _():
        c1_ref[0] = cst[...]
        m1_ref[0] = mst[...]


def _mlstm(m3, gif, gift, conv0, c0aug, m0, mw, lc):
    bsz, t = m3.shape[:2]
    full = lambda a: pl.BlockSpec(a.shape, lambda b, c: (0,) * a.ndim)
    ws = [mw["conv_w"], mw["conv_b"], mw["wq"], mw["wk"], mw["wkt"], mw["norm_g"]]
    return pl.pallas_call(
        functools.partial(_mlstm_kernel, lc=lc),
        grid=(bsz, t // lc),
        in_specs=[pl.BlockSpec((1, lc, 3 * MW), lambda b, c: (b, c, 0)),
                  pl.BlockSpec((1, lc, LANES), lambda b, c: (b, c, 0)),
                  pl.BlockSpec((1, 8, lc), lambda b, c: (b, 0, c)),
                  pl.BlockSpec((1, 8, MW), lambda b, c: (b, 0, 0)),
                  pl.BlockSpec((1, MH, MD, 2 * MD), lambda b, c: (b, 0, 0, 0)),
                  pl.BlockSpec((1, 1, LANES), lambda b, c: (b, 0, 0))] + [full(a) for a in ws],
        out_specs=[pl.BlockSpec((1, lc, MW), lambda b, c: (b, c, 0)),
                   pl.BlockSpec((1, MH, MD, 2 * MD), lambda b, c: (b, 0, 0, 0)),
                   pl.BlockSpec((1, 1, LANES), lambda b, c: (b, 0, 0))],
        out_shape=[jax.ShapeDtypeStruct((bsz, t, MW), BF16),
                   jax.ShapeDtypeStruct((bsz, MH, MD, 2 * MD), F32),
                   jax.ShapeDtypeStruct((bsz, 1, LANES), F32)],
        scratch_shapes=[pltpu.VMEM((lc + 8, MW), F32), pltpu.VMEM((MH, MD, 2 * MD), F32),
                        pltpu.VMEM((1, LANES), F32)],
        compiler_params=pltpu.CompilerParams(dimension_semantics=("arbitrary", "arbitrary"),
                                             vmem_limit_bytes=VMEM_LIMIT),
        name="mlstm",
    )(m3, gif, gift, conv0, c0aug, m0, *ws)


def _memkv_kernel(x_ref, g_ref, wk_ref, wv_ref, k_out, v_out):
    h = _rms(x_ref[...], g_ref[...]).astype(BF16)
    k_out[...] = _dot(h, wk_ref[...])
    v_out[...] = _dot(h, wv_ref[...])


def _mem_kv(mem, g, wk, wv):
    n = mem.shape[0]
    return pl.pallas_call(
        _memkv_kernel,
        out_shape=[jax.ShapeDtypeStruct((n, CH * CD), F32), jax.ShapeDtypeStruct((n, CH * CD), F32)],
        compiler_params=pltpu.CompilerParams(vmem_limit_bytes=VMEM_LIMIT),
        name="mem_kv",
    )(mem, g, wk, wv)


def _finish_kernel(x_ref, nsa_ref, ml_ref, won_ref, wom_ref, gc_ref, wcq_ref, mk_ref, mv_ref, wco_ref, gf_ref,
                   wr_ref, br_ref, x2_out, h3_out, comb_out, x1_sc, q_sc, m_sc, l_sc, acc_sc, *, tm, tk, tok_per_b):
    ti = pl.program_id(0)
    ki = pl.program_id(1)

    @pl.when(ki == 0)
    def _():
        x1 = x_ref[...] + _dot(nsa_ref[...], won_ref[...]) + _dot(ml_ref[...], wom_ref[...])
        x1_sc[...] = x1
        h2 = _rms(x1, gc_ref[...]).astype(BF16)
        q_sc[...] = (_dot(h2, wcq_ref[...]) * (CD ** -0.5)).astype(BF16)
        m_sc[...] = jnp.full(m_sc.shape, NEG_INF, F32)
        l_sc[...] = jnp.zeros(l_sc.shape, F32)
        acc_sc[...] = jnp.zeros(acc_sc.shape, F32)

    row_b = (ti * tm + _iota((tm, 1), 0)) // tok_per_b
    key_b = (ki * tk + _iota((1, tk), 1)) // N_MEM
    mask = row_b == key_b
    for h in range(CH):
        hs = slice(h * CD, (h + 1) * CD)
        s = jnp.where(mask, _dot_nt(q_sc[:, hs], mk_ref[:, hs].astype(BF16)), NEG_INF)
        m_old = m_sc[h]
        m_new = jnp.maximum(m_old, jnp.max(s, axis=1, keepdims=True))
        m_safe = jnp.where(m_new == NEG_INF, 0.0, m_new)
        alpha = jnp.exp(m_old - m_safe)
        p = jnp.exp(s - m_safe)
        l_sc[h] = alpha * l_sc[h] + jnp.sum(p, axis=1, keepdims=True)
        acc_sc[:, hs] = alpha * acc_sc[:, hs] + _dot(p.astype(BF16), mv_ref[:, hs].astype(BF16))
        m_sc[h] = m_new

    @pl.when(ki == pl.num_programs(1) - 1)
    def _():
        o = jnp.concatenate([acc_sc[:, h * CD:(h + 1) * CD] / l_sc[h] for h in range(CH)], axis=1)
        x2 = x1_sc[...] + _dot(o.astype(BF16), wco_ref[...])
        x2_out[...] = x2
        h3 = _rms(x2, gf_ref[...])
        h3_out[...] = h3.astype(BF16)
        lg = _dot(h3.astype(BF16), wr_ref[...]) + br_ref[...]
        lane = _iota(lg.shape, 1)
        lane_f = lane.astype(F32)
        is_grp = (lane >= N_EXPERT) & (lane < N_EXPERT + N_GROUPS)
        gl = jnp.where(is_grp, lg, NEG_INF)
        gmax = jnp.max(gl, axis=1, keepdims=True)
        ge = jnp.exp(gl - gmax)
        pg_top = 1.0 / jnp.sum(ge, axis=1, keepdims=True)
        g_idx = jnp.min(jnp.where(gl == gmax, lane_f, 1.0e9), axis=1, keepdims=True) - N_EXPERT
        in_grp = (lane < N_EXPERT) & ((lane >> 3).astype(F32) == g_idx)
        el = jnp.where(in_grp, lg, NEG_INF)
        emax = jnp.max(el, axis=1, keepdims=True)
        ee = jnp.exp(el - emax)
        pe = ee / jnp.sum(ee, axis=1, keepdims=True)
        pe = jnp.where(in_grp, pe, -1.0)
        p1 = jnp.max(pe, axis=1, keepdims=True)
        i1 = jnp.min(jnp.where(pe == p1, lane_f, 1.0e9), axis=1, keepdims=True)
        pe2 = jnp.where(lane_f == i1, -1.0, pe)
        p2 = jnp.max(pe2, axis=1, keepdims=True)
        i2 = jnp.min(jnp.where(pe2 == p2, lane_f, 1.0e9), axis=1, keepdims=True)
        tot = p1 + p2
        comb_out[...] = pg_top * jnp.where(lane_f == i1, p1 / tot, jnp.where(lane_f == i2, p2 / tot, 0.0))


def _finish(x, nsa, ml, mk, mv, fw, tm, tk, tok_per_b):
    m = x.shape[0]
    nk = mk.shape[0] // tk
    full = lambda a: pl.BlockSpec(a.shape, lambda i, k: (0,) * a.ndim)
    rows = lambda n: pl.BlockSpec((tm, n), lambda i, k: (i, 0))
    keys = pl.BlockSpec((tk, CH * CD), lambda i, k: (k, 0))
    return pl.pallas_call(
        functools.partial(_finish_kernel, tm=tm, tk=tk, tok_per_b=tok_per_b),
        grid=(m // tm, nk),
        in_specs=[rows(D_MODEL), rows(512), rows(MW), full(fw["w_out_nsa"]), full(fw["w_out_ml"]), full(fw["g_cross"]),
                  full(fw["w_cq"]), keys, keys, full(fw["w_co"]), full(fw["g_ffn"]), full(fw["w_router"]),
                  full(fw["b_router"])],
        out_specs=[rows(D_MODEL), rows(D_MODEL), rows(LANES)],
        out_shape=[jax.ShapeDtypeStruct((m, D_MODEL), F32), jax.ShapeDtypeStruct((m, D_MODEL), BF16),
                   jax.ShapeDtypeStruct((m, LANES), F32)],
        scratch_shapes=[pltpu.VMEM((tm, D_MODEL), F32), pltpu.VMEM((tm, CH * CD), BF16),
                        pltpu.VMEM((CH, tm, 1), F32), pltpu.VMEM((CH, tm, 1), F32), pltpu.VMEM((tm, CH * CD), F32)],
        compiler_params=pltpu.CompilerParams(dimension_semantics=("arbitrary", "arbitrary"),
                                             vmem_limit_bytes=VMEM_LIMIT),
        name="finish",
    )(x, nsa, ml, fw["w_out_nsa"], fw["w_out_ml"], fw["g_cross"], fw["w_cq"], mk, mv, fw["w_co"], fw["g_ffn"],
      fw["w_router"], fw["b_router"])


def _moe_kernel(h_ref, comb_ref, x2_ref, w13_ref, w2_ref, gfin_ref, y_ref, acc):
    e = pl.program_id(1)

    @pl.when(e == 0)
    def _():
        acc[...] = jnp.zeros(acc.shape, F32)

    au = _dot(h_ref[...], w13_ref[0])
    comb = comb_ref[...]
    ce = jnp.sum(jnp.where(_iota(comb.shape, 1) == e, comb, 0.0), axis=1, keepdims=True)
    hid = _silu(au[:, :D_EXPERT]) * au[:, D_EXPERT:] * ce
    acc[...] += _dot(hid.astype(BF16), w2_ref[0])

    @pl.when(e == pl.num_programs(1) - 1)
    def _():
        y_ref[...] = _rms(x2_ref[...] + acc[...], gfin_ref[...])


def _moe(h3, comb, x2, w13, w2, g_final, tm):
    m = h3.shape[0]
    rows = lambda n: pl.BlockSpec((tm, n), lambda i, e: (i, 0))
    return pl.pallas_call(
        _moe_kernel,
        grid=(m // tm, N_EXPERT),
        in_specs=[rows(D_MODEL), rows(LANES), rows(D_MODEL),
                  pl.BlockSpec((1, D_MODEL, 2 * D_EXPERT), lambda i, e: (e, 0, 0)),
                  pl.BlockSpec((1, D_EXPERT, D_MODEL), lambda i, e: (e, 0, 0)),
                  pl.BlockSpec((1, D_MODEL), lambda i, e: (0, 0))],
        out_specs=rows(D_MODEL),
        out_shape=jax.ShapeDtypeStruct((m, D_MODEL), F32),
        scratch_shapes=[pltpu.VMEM((tm, D_MODEL), F32)],
        compiler_params=pltpu.CompilerParams(dimension_semantics=("arbitrary", "arbitrary"),
                                             vmem_limit_bytes=VMEM_LIMIT),
        name="moe",
    )(h3, comb, x2, w13, w2, g_final)


def _prep_in_weights(g_mix, w_in, b_in):
    w, b = w_in[0], b_in[0]
    o = [0, 512, 1280, 1304, 2840, 2848]
    head_sel = (jnp.arange(NSA_HEADS)[:, None] // GQA == jnp.arange(KVH)[None, :]).astype(F32)

    def pad_q(a):
        a = a.reshape(a.shape[:-1] + (NSA_HEADS, 1, DH)) * head_sel[:, :, None]
        return a.reshape(a.shape[:-3] + (NSA_HEADS * KVH * DH,))

    def pad_lanes(a, n):
        return jnp.pad(a, [(0, 0)] * (a.ndim - 1) + [(0, n - a.shape[-1])])

    wif = pad_lanes(w[:, o[4]:o[5]], LANES)
    bif = pad_lanes(b[o[4]:o[5]], LANES)
    return {
        "g_mix": g_mix.reshape(1, D_MODEL),
        "wq": pad_q(w[:, o[0]:o[1]]).astype(BF16), "bq": pad_q(b[o[0]:o[1]]).reshape(1, -1),
        "wkv": w[:, o[1]:o[2]].astype(BF16), "bkv": b[o[1]:o[2]].reshape(1, -1),
        "wg": pad_lanes(w[:, o[2]:o[3]], LANES).astype(BF16), "bg": pad_lanes(b[o[2]:o[3]], LANES).reshape(1, -1),
        "wm": w[:, o[3]:o[4]].astype(BF16), "bm": b[o[3]:o[4]].reshape(1, -1),
        "wif": wif.astype(BF16), "bif": bif.reshape(1, -1),
        "wift": wif[:, :16].T.astype(BF16), "bift": bif[:16].reshape(16, 1),
    }


def _prep_compress_weights(pe, w1, b1, w2):
    eye = jnp.eye(KVH, dtype=F32)
    wbd = jnp.einsum("pdf,hg->phdgf", w1, eye).reshape(CMP_BLOCK, KVH * DH, KVH * CMP_HIDDEN)
    wa, wb = wbd[:CMP_STRIDE], wbd[CMP_STRIDE:]
    wab = jnp.concatenate([wa, wb], axis=-1)
    wab = wab.reshape(CMP_STRIDE // 2, 2 * KVH * DH, 2 * KVH * CMP_HIDDEN)
    bias = b1 + jnp.einsum("pd,pdf->f", pe, w1, precision=lax.Precision.HIGHEST)
    bias = jnp.tile(bias, KVH).reshape(1, KVH * CMP_HIDDEN)
    w2bd = jnp.einsum("fd,hg->hfgd", w2, eye).reshape(KVH * CMP_HIDDEN, KVH * DH)
    return wab.astype(BF16), bias, w2bd.astype(BF16)


def _sel_matrix(nc, n_blocks_padded):
    c = jnp.arange(nc)[:, None]
    j = jnp.arange(n_blocks_padded)[None, :]
    ok = (c >= 4 * j - 1) & (c <= 4 * j + 3) & (c < nc - 1)
    return ok.astype(BF16)


def _round_up(x, m):
    return (x + m - 1) // m * m


def kernel(x_prompt, x_sample, cache_cmp_k, cache_cmp_v, cache_sel_k, cache_sel_v, cache_win_k, cache_win_v, state_mlstm_c, state_mlstm_n, state_mlstm_m, state_conv, cache_mem_k, cache_mem_v, page_table, mem_prompt, g_mix, w_in, b_in, ck_pe, ck_w1, ck_b1, ck_w2, cv_pe, cv_w1, cv_b1, cv_w2, conv_w, conv_b, m_wq, m_wk, m_norm_g, w_out, g_cross, g_mem, w_cq, w_ck, w_cv, w_co, g_ffn, w_group, b_group, w_expert, b_expert, e_w1, e_w3, e_w2, g_final):
    t = x_prompt.shape[1]
    bsz, n_tok = x_sample.shape[:2]
    n_pages = page_table.shape[1]
    past = n_pages * PAGE
    n_pool = cache_cmp_k.shape[1]
    assert x_prompt.shape[0] == 1 and g_mix.shape[0] == 1 and t % SEL_TILE == 0 and t // SEL_TILE <= 32
    assert n_tok <= 8 and cache_win_k.shape[2] == WINDOW

    inw = _prep_in_weights(g_mix[0], w_in, b_in)
    ka, kb, k2 = _prep_compress_weights(ck_pe[0], ck_w1[0], ck_b1[0], ck_w2[0])
    va, vb, v2 = _prep_compress_weights(cv_pe[0], cv_w1[0], cv_b1[0], cv_w2[0])
    cw = {"wab": jnp.stack([ka, va]), "bias": jnp.stack([kb, vb]), "w2": jnp.stack([k2, v2])}
    mw = {"conv_w": conv_w[0], "conv_b": conv_b[0].reshape(1, MW), "wq": m_wq[0].astype(BF16),
          "wk": m_wk[0].astype(BF16), "wkt": jnp.swapaxes(m_wk[0], 1, 2).astype(BF16),
          "norm_g": m_norm_g[0].reshape(1, MW)}
    wo = w_out[0]
    wo_nsa = wo[:NSA_HEADS * DH].reshape(KVH, GQA, DH, D_MODEL).transpose(1, 0, 2, 3).reshape(NSA_HEADS * DH, D_MODEL)
    w_router = jnp.pad(jnp.concatenate([w_expert[0], w_group[0]], axis=1), ((0, 0), (0, LANES - N_EXPERT - N_GROUPS)))
    b_router = jnp.pad(jnp.concatenate([b_expert[0], b_group[0]]), (0, LANES - N_EXPERT - N_GROUPS)).reshape(1, LANES)
    fw = {"w_out_nsa": wo_nsa.astype(BF16), "w_out_ml": wo[NSA_HEADS * DH:].astype(BF16),
          "g_cross": g_cross[0].reshape(1, -1), "w_cq": w_cq[0].astype(BF16), "w_co": w_co[0].astype(BF16),
          "g_ffn": g_ffn[0].reshape(1, -1), "w_router": w_router.astype(BF16), "b_router": b_router}
    w13 = jnp.concatenate([e_w1[0], e_w3[0]], axis=-1).reshape(N_EXPERT, D_MODEL, 2 * D_EXPERT).astype(BF16)
    w2e = e_w2[0].reshape(N_EXPERT, D_EXPERT, D_MODEL).astype(BF16)
    gfin = g_final.reshape(1, D_MODEL)

    xp = x_prompt[0]
    q_p, kvf_p, kvb_p, gate_p, m3_p, gif_p, gift_p = _in_projection(xp, inw, 512)
    cmp_p = _compress_prompt(kvf_p, cw)
    msel_p = _sel_matrix(t // CMP_STRIDE, _round_up(t // SEL_BLOCK, LANES))
    nsa_p = _nsa_prompt(q_p, gate_p, cmp_p[0].astype(BF16), cmp_p[1].astype(BF16), kvb_p, msel_p)
    lc_p = 256 if t % 256 == 0 else t
    ml_p, c1_p, m1_p = _mlstm(m3_p[None], gif_p[None], gift_p[None], jnp.zeros((1, 8, MW), F32),
                              jnp.zeros((1, MH, MD, 2 * MD), F32), jnp.zeros((1, 1, LANES), F32), mw, lc_p)
    mk_p, mv_p = _mem_kv(mem_prompt[0], g_mem[0].reshape(1, -1), w_ck[0].astype(BF16), w_cv[0].astype(BF16))
    x2_p, h3_p, comb_p = _finish(xp, nsa_p, ml_p[0], mk_p, mv_p, fw, 512, N_MEM, t)
    y_p = _moe(h3_p, comb_p, x2_p, w13, w2e, gfin, 1024)

    kv5 = lambda a: a.reshape(1, 1, -1, KVH, DH)
    wb = min(WINDOW, t)
    prompt_state = (
        kv5(kvf_p[:, 0:128]), kv5(kvf_p[:, 128:256]), kv5(kvf_p[:, 256:384]), kv5(kvf_p[:, 384:512]),
        kv5(kvf_p[t - wb:, 512:640]), kv5(kvf_p[t - wb:, 640:768]),
        c1_p[:, :, :, :MD][None], c1_p[:, :, :, MD][None], m1_p[:, 0, :MH][None],
        m3_p[None, None, t - (CONV_W - 1):, :MW],
        mk_p.reshape(1, 1, N_MEM, CH, CD), mv_p.reshape(1, 1, N_MEM, CH, CD))

    n_s = bsz * n_tok
    xs = x_sample.reshape(n_s, D_MODEL)
    q_s, kvf_s, _, gate_s, m3_s, gif_s, gift_s = _in_projection(xs, inw, n_s)
    q_dec = q_s.reshape(bsz, n_tok, KVH, GQA, LANES).transpose(0, 2, 1, 3, 4)
    q_dec = jnp.pad(q_dec, ((0, 0), (0, 0), (0, 0), (0, 8 - GQA), (0, 0))).reshape(bsz, KVH, 8 * n_tok, LANES)
    g_dec = gate_s[:, :NSA_HEADS * 3].reshape(bsz, n_tok, KVH, GQA, 3).transpose(0, 2, 1, 3, 4)
    g_dec = jnp.pad(g_dec, ((0, 0), (0, 0), (0, 0), (0, 8 - GQA), (0, LANES - 3))).reshape(bsz, KVH, 8 * n_tok, LANES)
    pool2d = lambda a: a.reshape(n_pool, PAGE, KVH * DH)
    ns = past // SEL_BLOCK + 1
    msel_s = _sel_matrix(past // CMP_STRIDE, _round_up(ns, LANES))
    o_cmp, idx = _nsa_dec1(page_table, pool2d(cache_cmp_k), pool2d(cache_cmp_v), q_dec, cw, msel_s)
    new_rows = lambda c: jnp.pad(kvf_s[:, c * LANES:(c + 1) * LANES].reshape(bsz, n_tok, LANES),
                                 ((0, 0), (0, 8 - n_tok), (0, 0)))
    win_k = cache_win_k.reshape(bsz, WINDOW, LANES)
    win_v = cache_win_v.reshape(bsz, WINDOW, LANES)
    idx_flat = idx.reshape(bsz, KVH, 8, LANES)[:, :, :n_tok, :N_SEL].reshape(-1)
    o_dec = _nsa_dec2(page_table, idx_flat, pool2d(cache_sel_k), pool2d(cache_sel_v), q_dec, g_dec, o_cmp,
                      new_rows(2), new_rows(3), win_k, win_v, new_rows(4), new_rows(5))
    o_dec = o_dec.reshape(bsz, KVH, n_tok, 8, KVH, DH)[:, :, :, :GQA]
    nsa_s = jnp.stack([o_dec[:, h, :, :, h, :] for h in range(KVH)], axis=3).reshape(n_s, NSA_HEADS * DH)

    pad_tok = lambda a: jnp.pad(a.reshape((bsz, n_tok) + a.shape[1:]), ((0, 0), (0, 8 - n_tok), (0, 0)))
    gif_d = pad_tok(gif_s)
    gift_d = gift_s.reshape(8, bsz, n_tok).transpose(1, 0, 2)
    pad_gate = jnp.where(jnp.arange(8)[:, None] < MH, -1.0e30, 0.0)
    gif_d = jnp.where((jnp.arange(8)[:, None] >= n_tok) & (jnp.arange(LANES)[None, :] < MH), -1.0e30,
                      jnp.where(jnp.arange(8)[:, None] >= n_tok, 0.0, gif_d))
    gift_d = jnp.concatenate([gift_d, jnp.broadcast_to(pad_gate[None], (bsz, 8, 8 - n_tok))], axis=2)
    conv0 = jnp.pad(state_conv[0], ((0, 0), (8 - (CONV_W - 1), 0), (0, 0)))
    c0aug = jnp.concatenate([state_mlstm_c[0], state_mlstm_n[0][..., None],
                             jnp.zeros((bsz, MH, MD, MD - 1), F32)], axis=-1)
    m0 = jnp.pad(state_mlstm_m[0], ((0, 0), (0, LANES - MH))).reshape(bsz, 1, LANES)
    ml_s, c1_s, m1_s = _mlstm(pad_tok(m3_s), gif_d, gift_d, conv0, c0aug, m0, mw, 8)
    ml_s = ml_s[:, :n_tok].reshape(n_s, MW)
    mk_s = cache_mem_k[0].reshape(bsz * N_MEM, CH * CD)
    mv_s = cache_mem_v[0].reshape(bsz * N_MEM, CH * CD)
    x2_s, h3_s, comb_s = _finish(xs, nsa_s.astype(BF16), ml_s, mk_s, mv_s, fw, n_s, 4 * N_MEM, n_tok)
    y_s = _moe(h3_s, comb_s, x2_s, w13, w2e, gfin, n_s)

    new5 = lambda c: kvf_s[:, c * LANES:(c + 1) * LANES].reshape(1, bsz, n_tok, KVH, DH)
    roll_win = lambda old, c: jnp.concatenate([old[:, :, n_tok:], new5(c)], axis=2)
    conv_ext = jnp.concatenate([state_conv[0], m3_s[:, :MW].reshape(bsz, n_tok, MW)], axis=1)
    sample_state = (
        new5(0), new5(1), new5(2), new5(3), roll_win(cache_win_k, 4), roll_win(cache_win_v, 5),
        c1_s[:, :, :, :MD][None], c1_s[:, :, :, MD][None], m1_s[:, 0, :MH][None], conv_ext[None, :, n_tok:])

    return (y_p[None], y_s.reshape(bsz, n_tok, D_MODEL)) + prompt_state + sample_state
```

```python
import functools
import math

import jax
import jax.numpy as jnp
from jax import lax
from jax.experimental import pallas as pl
from jax.experimental.pallas import tpu as pltpu

F32 = jnp.float32
BF16 = jnp.bfloat16
I32 = jnp.int32

D_MODEL = 1024
PAGE = 128
DH = 64
NSA_HEADS = 8
KVH = 2
GQA = 4
CMP_STRIDE = 16
CMP_BLOCK = 32
CMP_HIDDEN = 128
SEL_BLOCK = 64
N_SEL = 16
WINDOW = 512
FORCE_BONUS = 1.0e4
Q_BLOCK = 128
MH = 4
MD = 128
MW = MH * MD
CONV_W = 4
N_MEM = 256
CH = 4
CD = 128
N_EXPERT = 32
EPG = 8
N_GROUPS = 4
D_EXPERT = 256
EPS = 1e-6
LANES = 128
SEL_TILE = 1024
VMEM_LIMIT = 56 * 1024 * 1024

NEG_INF = float("-inf")


def _iota(shape, dim):
    return lax.broadcasted_iota(I32, shape, dim)


def _dot(a, b):
    return jnp.dot(a, b, preferred_element_type=F32)


def _dot_nt(a, b):
    return lax.dot_general(a, b, (((1,), (1,)), ((), ())), preferred_element_type=F32)


def _rms(x, g):
    return x * lax.rsqrt(jnp.mean(x * x, axis=-1, keepdims=True) + EPS) * g


def _sigmoid(x):
    return 1.0 / (1.0 + jnp.exp(-x))


def _silu(x):
    return x * _sigmoid(x)


def _log_sigmoid(x):
    return jnp.minimum(x, 0.0) - jnp.log(1.0 + jnp.exp(-jnp.abs(x)))


def _gelu_tanh(x):
    c = math.sqrt(2.0 / math.pi)
    return 0.5 * x * (1.0 + jnp.tanh(c * (x + 0.044715 * (x * x * x))))


def _masked_softmax(s, mask):
    s = jnp.where(mask, s, NEG_INF)
    m = jnp.max(s, axis=-1, keepdims=True)
    m = jnp.where(m == NEG_INF, 0.0, m)
    e = jnp.where(mask, jnp.exp(s - m), 0.0)
    den = jnp.sum(e, axis=-1, keepdims=True)
    return e / jnp.where(den > 0, den, 1.0)


def _split3_dot(x, w):
    hi = x.astype(BF16)
    r1 = x - hi.astype(F32)
    mid = r1.astype(BF16)
    lo = (r1 - mid.astype(F32)).astype(BF16)
    return _dot(hi, w) + _dot(mid, w) + _dot(lo, w)


def _slope_col(row_g, h):
    sl = [2.0 ** (-(GQA * h + g + 1.0)) for g in range(GQA)]
    return jnp.where(row_g == 0, sl[0], jnp.where(row_g == 1, sl[1], jnp.where(row_g == 2, sl[2], sl[3])))


def _topk_rounds(score):
    sel = jnp.zeros(score.shape, F32)
    blk = _iota(score.shape, 1)
    cols = []
    sc = score
    for _ in range(N_SEL):
        m = jnp.max(sc, axis=1, keepdims=True)
        idx = jnp.min(jnp.where(sc == m, blk, 1 << 30), axis=1, keepdims=True)
        pick = blk == idx
        sel = jnp.where(pick, 1.0, sel)
        sc = jnp.where(pick, NEG_INF, sc)
        cols.append(idx)
    return sel, cols


def _sel_scores(imp_slc, qpos_col, n_blocks):
    blk = _iota(imp_slc.shape, 1)
    cur = qpos_col >> 6
    valid = (blk * SEL_BLOCK) <= qpos_col
    forced = (blk == 0) | (blk == cur) | (blk == cur - 1)
    score = jnp.where(valid, imp_slc + jnp.where(forced, FORCE_BONUS, 0.0), -1.0)
    return jnp.where(blk < n_blocks, score, NEG_INF)


def _inproj_kernel(x_ref, g_ref, wq_ref, bq_ref, wkv_ref, bkv_ref, wg_ref, bg_ref, wm_ref, bm_ref,
                   wif_ref, bif_ref, wift_ref, bift_ref,
                   q_out, kvf_out, kvb_out, gate_out, m3_out, gif_out, gift_out):
    h = _rms(x_ref[...], g_ref[...]).astype(BF16)
    q_out[...] = ((_dot(h, wq_ref[...]) + bq_ref[...]) * (DH ** -0.5)).astype(BF16)
    kv = _dot(h, wkv_ref[...]) + bkv_ref[...]
    kvf_out[...] = kv
    kvb_out[...] = kv.astype(BF16)
    gate_out[...] = _sigmoid(_dot(h, wg_ref[...]) + bg_ref[...])
    m3_out[...] = _dot(h, wm_ref[...]) + bm_ref[...]
    gi = _dot(h, wif_ref[...]) + bif_ref[...]
    lane = _iota(gi.shape, 1)
    gif_out[...] = jnp.where((lane >= MH) & (lane < 2 * MH), _log_sigmoid(gi), gi)
    git = (_dot_nt(wift_ref[...], h) + bift_ref[...])[0:8]
    row = _iota(git.shape, 0)
    gift_out[...] = jnp.where(row >= MH, _log_sigmoid(git), git)


def _in_projection(x, wts, tm):
    m = x.shape[0]
    full = lambda a: pl.BlockSpec(a.shape, lambda i: (0,) * a.ndim)
    rows = lambda n: pl.BlockSpec((tm, n), lambda i: (i, 0))
    names = ["g_mix", "wq", "bq", "wkv", "bkv", "wg", "bg", "wm", "bm", "wif", "bif", "wift", "bift"]
    ws = [wts[n] for n in names]
    return pl.pallas_call(
        _inproj_kernel,
        grid=(m // tm,),
        in_specs=[rows(D_MODEL)] + [full(a) for a in ws],
        out_specs=[rows(1024), rows(768), rows(768), rows(LANES), rows(3 * MW), rows(LANES),
                   pl.BlockSpec((8, tm), lambda i: (0, i))],
        out_shape=[jax.ShapeDtypeStruct((m, 1024), BF16), jax.ShapeDtypeStruct((m, 768), F32),
                   jax.ShapeDtypeStruct((m, 768), BF16), jax.ShapeDtypeStruct((m, LANES), F32),
                   jax.ShapeDtypeStruct((m, 3 * MW), F32), jax.ShapeDtypeStruct((m, LANES), F32),
                   jax.ShapeDtypeStruct((8, m), F32)],
        compiler_params=pltpu.CompilerParams(dimension_semantics=("arbitrary",), vmem_limit_bytes=VMEM_LIMIT),
        name="in_projection",
    )(x, *ws)


def _compress_core(x_ref, wab_ref, bias, w2_ref, nchunk):
    acc = None
    for pp in range(CMP_STRIDE // 2):
        xa = x_ref[pl.ds(2 * pp, nchunk, stride=CMP_STRIDE), :].astype(BF16)
        xb = x_ref[pl.ds(2 * pp + 1, nchunk, stride=CMP_STRIDE), :].astype(BF16)
        d = _dot(jnp.concatenate([xa, xb], axis=1), wab_ref[pp])
        acc = d if acc is None else acc + d
    hw = KVH * CMP_HIDDEN
    z = acc[:, :hw] + pltpu.roll(acc[:, hw:], nchunk - 1, 0) + bias
    return _dot(_gelu_tanh(z).astype(BF16), w2_ref[...])


def _compress_kernel(x_ref, wab_ref, bias_ref, w2_ref, out_ref, *, nchunk):
    out_ref[0] = _compress_core(x_ref, wab_ref.at[0], bias_ref[0], w2_ref.at[0], nchunk)


def _compress_prompt(kvf, cw):
    t = kvf.shape[0]
    nchunk = t // CMP_STRIDE
    return pl.pallas_call(
        functools.partial(_compress_kernel, nchunk=nchunk),
        grid=(2,),
        in_specs=[pl.BlockSpec((t, LANES), lambda i: (0, i)),
                  pl.BlockSpec((1,) + cw["wab"].shape[1:], lambda i: (i, 0, 0, 0)),
                  pl.BlockSpec((1, 1, 256), lambda i: (i, 0, 0)),
                  pl.BlockSpec((1, 256, LANES), lambda i: (i, 0, 0))],
        out_specs=pl.BlockSpec((1, nchunk, LANES), lambda i: (i, 0, 0)),
        out_shape=jax.ShapeDtypeStruct((2, nchunk, LANES), F32),
        compiler_params=pltpu.CompilerParams(dimension_semantics=("arbitrary",), vmem_limit_bytes=VMEM_LIMIT),
        name="compress_prompt",
    )(kvf, cw["wab"], cw["bias"], cw["w2"])


def _head_slope(h, g):
    return 2.0 ** (-(GQA * h + g + 1.0))


def _dense_branch(q, k, v, h, kposf, maskneg):
    s_all = _dot_nt(q, k)
    es, invs, ps = [], [], []
    for g in range(GQA):
        s = s_all[g * Q_BLOCK:(g + 1) * Q_BLOCK] + _head_slope(h, g) * kposf + maskneg
        m = jnp.max(s, axis=1, keepdims=True)
        e = jnp.exp(s - jnp.where(m == NEG_INF, 0.0, m))
        den = jnp.sum(e, axis=1, keepdims=True)
        inv = 1.0 / jnp.where(den > 0, den, 1.0)
        es.append(e.astype(BF16))
        invs.append(inv)
        ps.append(e * inv)
    o = _dot(jnp.concatenate(es, axis=0), v) * jnp.concatenate(invs, axis=0)
    return o, ps


def _nsa_prompt_kernel(q_ref, gate_ref, kc_ref, vc_ref, ks_ref, vs_ref, kw_ref, vw_ref, msel_ref, exp_ref, out_ref,
                       sel_ref, m_ref, l_ref, acc_ref, *, t_len, exp_off):
    i = pl.program_id(0)
    qs = i * Q_BLOCK
    nb = t_len // SEL_BLOCK
    nbp = msel_ref.shape[1]
    nc = t_len // CMP_STRIDE
    qpos_q = qs + _iota((Q_BLOCK, 1), 0)

    qh, o_cmp, scores = [], [], []
    cidx = _iota((1, nc), 1)
    cpos = cidx * CMP_STRIDE + (CMP_BLOCK - 1)
    cmask = jnp.where((cpos <= qpos_q) & (cidx < nc - 1), 0.0, NEG_INF)
    cposf = (cpos - qs).astype(F32)
    for h in range(KVH):
        q = jnp.concatenate([q_ref[:, (GQA * h + g) * LANES:(GQA * h + g + 1) * LANES] for g in range(GQA)], axis=0)
        qh.append(q)
        o, ps = _dense_branch(q, kc_ref[...], vc_ref[...], h, cposf, cmask)
        o_cmp.append(o)
        imp = (ps[0] + ps[1]) + (ps[2] + ps[3])
        scores.append(_sel_scores(_split3_dot(imp, msel_ref[...]), qpos_q, nb))

    o_win = []
    wlen = WINDOW + Q_BLOCK
    w0 = pl.multiple_of(jnp.maximum(qs - WINDOW, 0), Q_BLOCK)
    wpos = w0 + _iota((1, wlen), 1)
    wmask = jnp.where((wpos <= qpos_q) & (wpos > qpos_q - WINDOW), 0.0, NEG_INF)
    for h in range(KVH):
        o, _ = _dense_branch(qh[h], kw_ref[pl.ds(w0, wlen), :], vw_ref[pl.ds(w0, wlen), :], h,
                             (wpos - qs).astype(F32), wmask)
        o_win.append(o)

    sel, _ = _topk_rounds(jnp.concatenate(scores, axis=0))
    sel_ref[...] = sel.astype(BF16)

    blocks_per_tile = SEL_TILE // SEL_BLOCK
    grp = jnp.where((_iota((nbp, LANES), 0) // blocks_per_tile) == _iota((nbp, LANES), 1), 1.0, 0.0).astype(BF16)
    lane = _iota((16, LANES), 1)
    n_causal = i // (SEL_TILE // Q_BLOCK) + 1

    o_sel = []
    for h in range(KVH):
        any_blk = jnp.max(sel[h * Q_BLOCK:(h + 1) * Q_BLOCK], axis=0, keepdims=True)
        cnt = _dot(jnp.broadcast_to(any_blk, (16, nbp)).astype(BF16), grp)
        hit = (cnt > 0.5) & (_iota((16, LANES), 0) == 0)
        lo = jnp.sum(jnp.where(hit & (lane < 16), (1 << (lane & 15)).astype(F32), 0.0)).astype(I32)
        hi = jnp.sum(jnp.where(hit & (lane >= 16) & (lane < 32), (1 << (lane & 15)).astype(F32), 0.0)).astype(I32)

        m_ref[...] = jnp.full(m_ref.shape, NEG_INF, F32)
        l_ref[...] = jnp.zeros(l_ref.shape, F32)
        acc_ref[...] = jnp.zeros(acc_ref.shape, F32)

        def tile(t, carry, h=h, lo=lo, hi=hi):
            word = jnp.where(t < 16, lo, hi)
            active = ((word >> (t & 15)) & 1) == 1

            @pl.when(active)
            def _():
                k0 = pl.multiple_of(t * SEL_TILE, SEL_TILE)
                kt = ks_ref[pl.ds(k0, SEL_TILE), :]
                vt = vs_ref[pl.ds(k0, SEL_TILE), :]
                kpos = k0 + _iota((1, SEL_TILE), 1)
                par = t & 1
                start = pl.multiple_of(exp_off - blocks_per_tile * (t - par), 2 * blocks_per_tile)
                mk = _dot(sel_ref[h * Q_BLOCK:(h + 1) * Q_BLOCK, :], exp_ref[par, pl.ds(start, nbp), :])
                maskneg = jnp.where((mk > 0.5) & (kpos <= qpos_q), 0.0, NEG_INF)
                kposf = (kpos - qs).astype(F32)
                s_all = _dot_nt(qh[h], kt)
                es, alphas = [], []
                for g in range(GQA):
                    rs = slice(g * Q_BLOCK, (g + 1) * Q_BLOCK)
                    s = s_all[rs] + _head_slope(h, g) * kposf + maskneg
                    m_old = m_ref[rs]
                    m_new = jnp.maximum(m_old, jnp.max(s, axis=1, keepdims=True))
                    m_safe = jnp.where(m_new == NEG_INF, 0.0, m_new)
                    alpha = jnp.exp(m_old - m_safe)
                    e = jnp.exp(s - m_safe)
                    l_ref[rs] = alpha * l_ref[rs] + jnp.sum(e, axis=1, keepdims=True)
                    m_ref[rs] = m_new
                    es.append(e.astype(BF16))
                    alphas.append(alpha)
                acc_ref[...] = jnp.concatenate(alphas, axis=0) * acc_ref[...] + _dot(jnp.concatenate(es, axis=0), vt)

            return carry

        lax.fori_loop(0, n_causal, tile, 0)
        l = l_ref[...]
        o_sel.append(acc_ref[...] / jnp.where(l > 0, l, 1.0))

    gates = gate_ref[...]
    lane_q = _iota((Q_BLOCK, LANES), 1)
    for g in range(GQA):
        parts = []
        sl = slice(g * Q_BLOCK, (g + 1) * Q_BLOCK)
        for h in range(KVH):
            c = 3 * (GQA * h + g)
            parts.append(gates[:, c:c + 1] * o_cmp[h][sl] + gates[:, c + 1:c + 2] * o_sel[h][sl]
                         + gates[:, c + 2:c + 3] * o_win[h][sl])
        out_ref[:, g * LANES:(g + 1) * LANES] = jnp.where(lane_q < DH, parts[0], parts[1]).astype(BF16)


def _expansion_table(n_tiles, nbp):
    bpt = SEL_TILE // SEL_BLOCK
    off = _round_up(bpt * (n_tiles - 1), 2 * bpt)
    r = jnp.arange(off + nbp)[None, :, None]
    p = jnp.arange(2)[:, None, None]
    k = jnp.arange(SEL_TILE)[None, None, :]
    return ((r - off - bpt * p) == (k // SEL_BLOCK)).astype(BF16), off


def _nsa_prompt(q_pad, gates, kcmp, vcmp, kvb, msel):
    t = q_pad.shape[0]
    nbp = msel.shape[1]
    expand, exp_off = _expansion_table(t // SEL_TILE, nbp)
    once = pl.Buffered(1)
    col = lambda c: pl.BlockSpec((t, LANES), lambda i, c=c: (0, c), pipeline_mode=once)
    full = lambda a: pl.BlockSpec(a.shape, lambda i: (0,) * a.ndim, pipeline_mode=once)
    return pl.pallas_call(
        functools.partial(_nsa_prompt_kernel, t_len=t, exp_off=exp_off),
        grid=(t // Q_BLOCK,),
        in_specs=[pl.BlockSpec((Q_BLOCK, 1024), lambda i: (i, 0)), pl.BlockSpec((Q_BLOCK, LANES), lambda i: (i, 0)),
                  full(kcmp), full(vcmp), col(2), col(3), col(4), col(5), full(msel), full(expand)],
        out_specs=pl.BlockSpec((Q_BLOCK, 512), lambda i: (i, 0)),
        out_shape=jax.ShapeDtypeStruct((t, 512), BF16),
        scratch_shapes=[pltpu.VMEM((KVH * Q_BLOCK, nbp), BF16), pltpu.VMEM((GQA * Q_BLOCK, 1), F32),
                        pltpu.VMEM((GQA * Q_BLOCK, 1), F32), pltpu.VMEM((GQA * Q_BLOCK, LANES), F32)],
        compiler_params=pltpu.CompilerParams(dimension_semantics=("arbitrary",), vmem_limit_bytes=VMEM_LIMIT),
        name="nsa_prompt",
    )(q_pad, gates, kcmp, vcmp, kvb, kvb, kvb, kvb, msel, expand)


def _nsa_dec1_kernel(pt_ref, ckp_ref, cvp_ref, q_ref, wab_ref, bias_ref, w2_ref, msel_ref,
                     ocmp_ref, idx_ref, stage, work, kc_sc, sem, *, n_pages, n_tok):
    i = pl.program_id(0)
    pool = i % 2
    b = i // 2
    past = n_pages * PAGE
    nc = past // CMP_STRIDE
    ns = past // SEL_BLOCK + 1

    def issue(item):
        slot = item % 2

        def issue_from(src_ref):
            def one(j, c):
                pg = pt_ref[(item // 2) * n_pages + j]
                pltpu.make_async_copy(src_ref.at[pg], stage.at[slot, j], sem.at[slot]).start()
                return c
            lax.fori_loop(0, n_pages, one, 0)

        @pl.when(slot == 0)
        def _():
            issue_from(ckp_ref)

        @pl.when(slot == 1)
        def _():
            issue_from(cvp_ref)

    @pl.when(i == 0)
    def _():
        issue(i)

    @pl.when(i + 1 < pl.num_programs(0))
    def _():
        issue(i + 1)

    def wait_one(j, c):
        pltpu.make_async_copy(ckp_ref.at[0], stage.at[pool, j], sem.at[pool]).wait()
        return c
    lax.fori_loop(0, n_pages, wait_one, 0)

    def transpose_one(j, c):
        work[pl.ds(pl.multiple_of(j * PAGE, PAGE), PAGE), :] = stage[pool, j].T
        return c
    lax.fori_loop(0, n_pages, transpose_one, 0)

    cmp = _compress_core(work, wab_ref.at[pool], bias_ref[pool], w2_ref.at[pool], nc).astype(BF16)

    @pl.when(pool == 0)
    def _():
        kc_sc[...] = cmp

    @pl.when(pool == 1)
    def _():
        kc = kc_sc[...]
        vc = cmp
        rows = 8 * n_tok
        row = _iota((rows, 1), 0)
        row_g = row & 7
        qpos = past + (row >> 3)
        cidx = _iota((1, nc), 1)
        dist = qpos - (cidx * CMP_STRIDE + (CMP_BLOCK - 1))
        cmask = (dist >= 0) & (cidx < nc - 1) & (row_g < GQA)
        qpos_t = past + _iota((8, 1), 0)
        scores = []
        for h in range(KVH):
            s = _dot_nt(q_ref[0, h], kc) - _slope_col(row_g, h) * dist.astype(F32)
            p = _masked_softmax(s, cmask)
            ocmp_ref[0, h] = _dot(p.astype(BF16), vc)
            imp = jnp.sum(p.reshape(n_tok, 8, nc), axis=1)
            imp = jnp.concatenate([imp, jnp.zeros((8 - n_tok, nc), F32)], axis=0) if n_tok < 8 else imp
            scores.append(_sel_scores(_split3_dot(imp, msel_ref[...]), qpos_t, ns))
        score = jnp.concatenate(scores, axis=0)
        _, cols = _topk_rounds(score)
        lane = _iota((KVH * 8, LANES), 1)
        idx = jnp.zeros((KVH * 8, LANES), I32)
        for k, c in enumerate(cols):
            idx = jnp.where(lane == k, c, idx)
        idx_ref[0] = idx


def _nsa_dec1(page_table, ck_pool_t, cv_pool_t, q_dec, cw, msel):
    bsz, n_pages = page_table.shape
    n_tok = q_dec.shape[2] // 8
    past = n_pages * PAGE
    full = lambda a: pl.BlockSpec(a.shape, lambda i, pt: (0,) * a.ndim)
    grid_spec = pltpu.PrefetchScalarGridSpec(
        num_scalar_prefetch=1,
        grid=(2 * bsz,),
        in_specs=[pl.BlockSpec(memory_space=pl.ANY), pl.BlockSpec(memory_space=pl.ANY),
                  pl.BlockSpec((1,) + q_dec.shape[1:], lambda i, pt: (i // 2, 0, 0, 0)),
                  full(cw["wab"]), full(cw["bias"]), full(cw["w2"]), full(msel)],
        out_specs=[pl.BlockSpec((1, KVH, 8 * n_tok, LANES), lambda i, pt: (i // 2, 0, 0, 0)),
                   pl.BlockSpec((1, KVH * 8, LANES), lambda i, pt: (i // 2, 0, 0))],
        scratch_shapes=[pltpu.VMEM((2, n_pages, KVH * DH, PAGE), F32), pltpu.VMEM((past, LANES), F32),
                        pltpu.VMEM((past // CMP_STRIDE, LANES), BF16), pltpu.SemaphoreType.DMA((2,))],
    )
    return pl.pallas_call(
        functools.partial(_nsa_dec1_kernel, n_pages=n_pages, n_tok=n_tok),
        grid_spec=grid_spec,
        out_shape=[jax.ShapeDtypeStruct((bsz, KVH, 8 * n_tok, LANES), F32),
                   jax.ShapeDtypeStruct((bsz, KVH * 8, LANES), I32)],
        compiler_params=pltpu.CompilerParams(dimension_semantics=("arbitrary",), vmem_limit_bytes=VMEM_LIMIT),
        name="nsa_decode_cmp",
    )(page_table.reshape(-1), ck_pool_t, cv_pool_t, q_dec, cw["wab"], cw["bias"], cw["w2"], msel)


def _nsa_dec2_kernel(pt_ref, idx_ref, skp_ref, svp_ref, q_ref, q64_ref, gate_ref, ocmp_ref, ksn_ref, vsn_ref,
                     wk_ref, wv_ref, kwn_ref, vwn_ref, out_ref, kg, vg, sem, *, n_pages, n_tok):
    b = pl.program_id(0)
    past = n_pages * PAGE
    ns = past // SEL_BLOCK + 1
    n_slot = KVH * n_tok * N_SEL
    glen = N_SEL * PAGE

    def copies(e):
        j = jnp.minimum(idx_ref[b * n_slot + e], ns - 2)
        pg = pt_ref[b * n_pages + (j >> 1)]
        src = pl.ds(pl.multiple_of((e // (n_tok * N_SEL)) * DH, DH), DH)
        return (pltpu.make_async_copy(skp_ref.at[pg, src], kg.at[e], sem.at[0]),
                pltpu.make_async_copy(svp_ref.at[pg, src], vg.at[e], sem.at[1]))

    def start(e, c):
        ck, cv = copies(e)
        ck.start()
        cv.start()
        return c

    def wait(e, c):
        ck, cv = copies(e)
        ck.wait()
        cv.wait()
        return c

    lax.fori_loop(0, n_slot, start, 0)

    rows = 8 * n_tok
    row = _iota((rows, 1), 0)
    row_g = row & 7
    qpos = past + (row >> 3)
    o_win = []
    wb = wk_ref.shape[2]
    dist_o = qpos - (past - wb + _iota((1, wb), 1))
    dist_n = qpos - (past + _iota((1, 8), 1))
    for h in range(KVH):
        sl = _slope_col(row_g, h)
        s_o = jnp.where((dist_o >= 0) & (dist_o < WINDOW),
                        _dot(q_ref[0, h], wk_ref[0].astype(BF16)) - sl * dist_o.astype(F32), NEG_INF)
        s_n = jnp.where((dist_n >= 0) & (dist_n < WINDOW),
                        _dot_nt(q_ref[0, h], kwn_ref[0].astype(BF16)) - sl * dist_n.astype(F32), NEG_INF)
        m = jnp.maximum(jnp.max(s_o, axis=1, keepdims=True), jnp.max(s_n, axis=1, keepdims=True))
        m = jnp.where(m == NEG_INF, 0.0, m)
        p_o = jnp.exp(s_o - m)
        p_n = jnp.exp(s_n - m)
        den = jnp.sum(p_o, axis=1, keepdims=True) + jnp.sum(p_n, axis=1, keepdims=True)
        num = (_dot_nt(p_o.astype(BF16), wv_ref[0].astype(BF16))
               + _dot(p_n.astype(BF16), vwn_ref[0].astype(BF16)))
        o_win.append(num / jnp.where(den > 0, den, 1.0))

    lax.fori_loop(0, n_slot, wait, 0)

    lane_g = _iota((1, glen), 1)
    ksn = ksn_ref[0].astype(BF16)
    vsn = vsn_ref[0].astype(BF16)
    for h in range(KVH):
        o_rows = []
        for t in range(n_tok):
            base = (h * n_tok + t) * N_SEL
            kpos = jnp.zeros((1, glen), I32)
            half = jnp.full((1, glen), 2, I32)
            has_new = jnp.zeros((), I32)
            for k in range(N_SEL):
                j = idx_ref[b * n_slot + base + k]
                is_new = j == ns - 1
                has_new = has_new | is_new.astype(I32)
                in_slot = (lane_g >> 7) == k
                kpos = jnp.where(in_slot, (j >> 1) * PAGE + (lane_g & (PAGE - 1)), kpos)
                half = jnp.where(in_slot, jnp.where(is_new, 2, j & 1), half)
            qp = past + t
            g8 = _iota((8, 1), 0)
            sl = _slope_col(g8, h)
            dist_s = qp - kpos
            kt = jnp.concatenate([kg[base + k] for k in range(N_SEL)], axis=1).astype(BF16)
            vt = jnp.concatenate([vg[base + k] for k in range(N_SEL)], axis=1).astype(BF16)
            s_s = jnp.where((dist_s >= 0) & (((lane_g >> 6) & 1) == half),
                            _dot(q64_ref[0, h, 8 * t:8 * t + 8, :], kt) - sl * dist_s.astype(F32), NEG_INF)
            dist_n = qp - (past + _iota((1, 8), 1))
            s_n = jnp.where((dist_n >= 0) & (has_new > 0),
                            _dot_nt(q_ref[0, h, 8 * t:8 * t + 8, :], ksn) - sl * dist_n.astype(F32), NEG_INF)
            m = jnp.maximum(jnp.max(s_s, axis=1, keepdims=True), jnp.max(s_n, axis=1, keepdims=True))
            m = jnp.where(m == NEG_INF, 0.0, m)
            p_s = jnp.exp(s_s - m)
            p_n = jnp.exp(s_n - m)
            den = jnp.sum(p_s, axis=1, keepdims=True) + jnp.sum(p_n, axis=1, keepdims=True)
            o_head = _dot_nt(p_s.astype(BF16), vt)
            num = jnp.concatenate([o_head] * KVH, axis=1) + _dot(p_n.astype(BF16), vsn)
            o_rows.append(num / jnp.where(den > 0, den, 1.0))
        o_sel = jnp.concatenate(o_rows, axis=0)
        gt = gate_ref[0, h]
        out_ref[0, h] = gt[:, 0:1] * ocmp_ref[0, h] + gt[:, 1:2] * o_sel + gt[:, 2:3] * o_win[h]


def _nsa_dec2(page_table, idx, sk_pool_t, sv_pool_t, q_dec, q64, g_dec, o_cmp, ks_new, vs_new, win_k_t, win_v_t,
              kw_new, vw_new):
    bsz, n_pages = page_table.shape
    n_tok = q_dec.shape[2] // 8
    blk4 = lambda a: pl.BlockSpec((1,) + a.shape[1:], lambda b, pt, ix: (b, 0, 0, 0))
    blk3 = lambda a: pl.BlockSpec((1,) + a.shape[1:], lambda b, pt, ix: (b, 0, 0))
    n_slot = KVH * n_tok * N_SEL
    grid_spec = pltpu.PrefetchScalarGridSpec(
        num_scalar_prefetch=2,
        grid=(bsz,),
        in_specs=[pl.BlockSpec(memory_space=pl.ANY), pl.BlockSpec(memory_space=pl.ANY),
                  blk4(q_dec), blk4(q64), blk4(g_dec), blk4(o_cmp), blk3(ks_new), blk3(vs_new),
                  blk3(win_k_t), blk3(win_v_t), blk3(kw_new), blk3(vw_new)],
        out_specs=pl.BlockSpec((1, KVH, 8 * n_tok, LANES), lambda b, pt, ix: (b, 0, 0, 0)),
        scratch_shapes=[pltpu.VMEM((n_slot, DH, PAGE), F32), pltpu.VMEM((n_slot, DH, PAGE), F32),
                        pltpu.SemaphoreType.DMA((2,))],
    )
    return pl.pallas_call(
        functools.partial(_nsa_dec2_kernel, n_pages=n_pages, n_tok=n_tok),
        grid_spec=grid_spec,
        out_shape=jax.ShapeDtypeStruct((bsz, KVH, 8 * n_tok, LANES), F32),
        compiler_params=pltpu.CompilerParams(dimension_semantics=("arbitrary",), vmem_limit_bytes=VMEM_LIMIT),
        name="nsa_decode_sel",
    )(page_table.reshape(-1), idx, sk_pool_t, sv_pool_t, q_dec, q64, g_dec, o_cmp, ks_new, vs_new, win_k_t, win_v_t,
      kw_new, vw_new)


def _mlstm_kernel(m3_ref, gif_ref, gift_ref, conv0_ref, c0_ref, m0_ref, cw_ref, cb_ref, wq_ref, wk_ref, wkt_ref,
                  ng_ref, out_ref, c1_ref, m1_ref, cbuf, cst, mst, *, lc):
    c = pl.program_id(1)

    @pl.when(c == 0)
    def _():
        cbuf[0:8, :] = conv0_ref[0]
        cst[...] = c0_ref[0]
        mst[...] = m0_ref[0]

    cbuf[8:8 + lc, :] = m3_ref[0, :, 0:MW]
    conv = cb_ref[...] + cbuf[5:5 + lc, :] * cw_ref[0:1, :]
    for j in range(1, CONV_W):
        conv = conv + cbuf[5 + j:5 + j + lc, :] * cw_ref[j:j + 1, :]
    cbuf[0:8, :] = cbuf[lc:lc + 8, :]
    csil = _silu(conv).astype(BF16)

    tri = _iota((lc, lc), 1) <= _iota((lc, lc), 0)
    tri_t = _iota((lc, lc), 0) <= _iota((lc, lc), 1)
    gif = gif_ref[0]
    gift = gift_ref[0]
    ones_col = jnp.where(_iota((lc, LANES), 1) == 0, 1.0, 0.0)
    m_all = mst[...]
    for h in range(MH):
        hs = slice(h * MD, (h + 1) * MD)
        ch = csil[:, hs]
        q = _dot(ch, wq_ref[h]).astype(BF16)
        k = (_dot(ch, wk_ref[h]) * (MD ** -0.5)).astype(BF16)
        kt = _dot_nt(wkt_ref[h], ch) * (MD ** -0.5)
        vaug = jnp.concatenate([m3_ref[0, :, MW + h * MD:MW + (h + 1) * MD], ones_col], axis=1).astype(BF16)
        lf_col = gif[:, MH + h:MH + h + 1]
        ig_row = gift[h:h + 1, :]
        lf_row = gift[MH + h:MH + h + 1, :]
        bcum_col = jnp.sum(jnp.where(tri, lf_row, 0.0), axis=1, keepdims=True)
        bcum_row = jnp.sum(jnp.where(tri_t, lf_col, 0.0), axis=0, keepdims=True)
        m_prev = m_all[0:1, h:h + 1]
        dlog = jnp.where(tri, bcum_col - bcum_row + ig_row, NEG_INF)
        inter = bcum_col + m_prev
        m_t = jnp.maximum(inter, jnp.max(dlog, axis=1, keepdims=True))
        w_intra = jnp.exp(dlog - m_t)
        w_inter = jnp.exp(inter - m_t)
        sw = (_dot_nt(q, k) * w_intra).astype(BF16)
        cprev = cst[h]
        numden = w_inter * _dot(q, cprev.astype(BF16)) + _dot(sw, vaug)
        den = numden[:, MD:MD + 1]
        hh = numden[:, 0:MD] / jnp.maximum(jnp.abs(den), jnp.exp(-m_t))
        b_last = bcum_col[lc - 1:lc, :]
        dl_end = b_last - bcum_row + ig_row
        m_new = jnp.maximum(b_last + m_prev, jnp.max(dl_end, axis=1, keepdims=True))
        a_prev = jnp.exp(b_last + m_prev - m_new)
        w_end = jnp.exp(dl_end - m_new)
        cst[h] = a_prev * cprev + _dot((kt * w_end).astype(BF16), vaug)
        m_all = jnp.where(_iota(m_all.shape, 1) == h, m_new, m_all)
        hn = hh * lax.rsqrt(jnp.mean(hh * hh, axis=-1, keepdims=True) + EPS) * ng_ref[:, hs]
        out_ref[0, :, hs] = (_sigmoid(m3_ref[0, :, 2 * MW + h * MD:2 * MW + (h + 1) * MD]) * hn).astype(BF16)
    mst[...] = m_all

    @pl.when(c == pl.num_programs(1) - 1)
    def _():
        c1_ref[0] = cst[...]
        m1_ref[0] = mst[...]


def _mlstm(m3, gif, gift, conv0, c0aug, m0, mw, lc):
    bsz, t = m3.shape[:2]
    full = lambda a: pl.BlockSpec(a.shape, lambda b, c: (0,) * a.ndim)
    ws = [mw["conv_w"], mw["conv_b"], mw["wq"], mw["wk"], mw["wkt"], mw["norm_g"]]
    return pl.pallas_call(
        functools.partial(_mlstm_kernel, lc=lc),
        grid=(bsz, t // lc),
        in_specs=[pl.BlockSpec((1, lc, 3 * MW), lambda b, c: (b, c, 0)),
                  pl.BlockSpec((1, lc, LANES), lambda b, c: (b, c, 0)),
                  pl.BlockSpec((1, 8, lc), lambda b, c: (b, 0, c)),
                  pl.BlockSpec((1, 8, MW), lambda b, c: (b, 0, 0)),
                  pl.BlockSpec((1, MH, MD, 2 * MD), lambda b, c: (b, 0, 0, 0)),
                  pl.BlockSpec((1, 1, LANES), lambda b, c: (b, 0, 0))] + [full(a) for a in ws],
        out_specs=[pl.BlockSpec((1, lc, MW), lambda b, c: (b, c, 0)),
                   pl.BlockSpec((1, MH, MD, 2 * MD), lambda b, c: (b, 0, 0, 0)),
                   pl.BlockSpec((1, 1, LANES), lambda b, c: (b, 0, 0))],
        out_shape=[jax.ShapeDtypeStruct((bsz, t, MW), BF16),
                   jax.ShapeDtypeStruct((bsz, MH, MD, 2 * MD), F32),
                   jax.ShapeDtypeStruct((bsz, 1, LANES), F32)],
        scratch_shapes=[pltpu.VMEM((lc + 8, MW), F32), pltpu.VMEM((MH, MD, 2 * MD), F32),
                        pltpu.VMEM((1, LANES), F32)],
        compiler_params=pltpu.CompilerParams(dimension_semantics=("arbitrary", "arbitrary"),
                                             vmem_limit_bytes=VMEM_LIMIT),
        name="mlstm",
    )(m3, gif, gift, conv0, c0aug, m0, *ws)


def _memkv_kernel(x_ref, g_ref, wk_ref, wv_ref, k_out, v_out):
    h = _rms(x_ref[...], g_ref[...]).astype(BF16)
    k_out[...] = _dot(h, wk_ref[...])
    v_out[...] = _dot(h, wv_ref[...])


def _mem_kv(mem, g, wk, wv):
    n = mem.shape[0]
    return pl.pallas_call(
        _memkv_kernel,
        out_shape=[jax.ShapeDtypeStruct((n, CH * CD), F32), jax.ShapeDtypeStruct((n, CH * CD), F32)],
        compiler_params=pltpu.CompilerParams(vmem_limit_bytes=VMEM_LIMIT),
        name="mem_kv",
    )(mem, g, wk, wv)


def _finish_kernel(x_ref, nsa_ref, ml_ref, won_ref, wom_ref, gc_ref, wcq_ref, mk_ref, mv_ref, wco_ref, gf_ref,
                   wr_ref, br_ref, x2_out, h3_out, comb_out, x1_sc, q_sc, m_sc, l_sc, acc_sc, *, tm, tk, tok_per_b):
    ti = pl.program_id(0)
    ki = pl.program_id(1)

    @pl.when(ki == 0)
    def _():
        x1 = x_ref[...] + _dot(nsa_ref[...], won_ref[...]) + _dot(ml_ref[...], wom_ref[...])
        x1_sc[...] = x1
        h2 = _rms(x1, gc_ref[...]).astype(BF16)
        q_sc[...] = (_dot(h2, wcq_ref[...]) * (CD ** -0.5)).astype(BF16)
        m_sc[...] = jnp.full(m_sc.shape, NEG_INF, F32)
        l_sc[...] = jnp.zeros(l_sc.shape, F32)
        acc_sc[...] = jnp.zeros(acc_sc.shape, F32)

    row_b = (ti * tm + _iota((tm, 1), 0)) // tok_per_b
    key_b = (ki * tk + _iota((1, tk), 1)) // N_MEM
    mask = row_b == key_b
    for h in range(CH):
        hs = slice(h * CD, (h + 1) * CD)
        s = jnp.where(mask, _dot_nt(q_sc[:, hs], mk_ref[:, hs].astype(BF16)), NEG_INF)
        m_old = m_sc[h]
        m_new = jnp.maximum(m_old, jnp.max(s, axis=1, keepdims=True))
        m_safe = jnp.where(m_new == NEG_INF, 0.0, m_new)
        alpha = jnp.exp(m_old - m_safe)
        p = jnp.exp(s - m_safe)
        l_sc[h] = alpha * l_sc[h] + jnp.sum(p, axis=1, keepdims=True)
        acc_sc[:, hs] = alpha * acc_sc[:, hs] + _dot(p.astype(BF16), mv_ref[:, hs].astype(BF16))
        m_sc[h] = m_new

    @pl.when(ki == pl.num_programs(1) - 1)
    def _():
        o = jnp.concatenate([acc_sc[:, h * CD:(h + 1) * CD] / l_sc[h] for h in range(CH)], axis=1)
        x2 = x1_sc[...] + _dot(o.astype(BF16), wco_ref[...])
        x2_out[...] = x2
        h3 = _rms(x2, gf_ref[...])
        h3_out[...] = h3.astype(BF16)
        lg = _dot(h3.astype(BF16), wr_ref[...]) + br_ref[...]
        lane = _iota(lg.shape, 1)
        lane_f = lane.astype(F32)
        is_grp = (lane >= N_EXPERT) & (lane < N_EXPERT + N_GROUPS)
        gl = jnp.where(is_grp, lg, NEG_INF)
        gmax = jnp.max(gl, axis=1, keepdims=True)
        ge = jnp.exp(gl - gmax)
        pg_top = 1.0 / jnp.sum(ge, axis=1, keepdims=True)
        g_idx = jnp.min(jnp.where(gl == gmax, lane_f, 1.0e9), axis=1, keepdims=True) - N_EXPERT
        in_grp = (lane < N_EXPERT) & ((lane >> 3).astype(F32) == g_idx)
        el = jnp.where(in_grp, lg, NEG_INF)
        emax = jnp.max(el, axis=1, keepdims=True)
        ee = jnp.exp(el - emax)
        pe = ee / jnp.sum(ee, axis=1, keepdims=True)
        pe = jnp.where(in_grp, pe, -1.0)
        p1 = jnp.max(pe, axis=1, keepdims=True)
        i1 = jnp.min(jnp.where(pe == p1, lane_f, 1.0e9), axis=1, keepdims=True)
        pe2 = jnp.where(lane_f == i1, -1.0, pe)
        p2 = jnp.max(pe2, axis=1, keepdims=True)
        i2 = jnp.min(jnp.where(pe2 == p2, lane_f, 1.0e9), axis=1, keepdims=True)
        tot = p1 + p2
        comb_out[...] = pg_top * jnp.where(lane_f == i1, p1 / tot, jnp.where(lane_f == i2, p2 / tot, 0.0))


def _finish(x, nsa, ml, mk, mv, fw, tm, tk, tok_per_b):
    m = x.shape[0]
    nk = mk.shape[0] // tk
    full = lambda a: pl.BlockSpec(a.shape, lambda i, k: (0,) * a.ndim)
    rows = lambda n: pl.BlockSpec((tm, n), lambda i, k: (i, 0))
    keys = pl.BlockSpec((tk, CH * CD), lambda i, k: (k, 0))
    return pl.pallas_call(
        functools.partial(_finish_kernel, tm=tm, tk=tk, tok_per_b=tok_per_b),
        grid=(m // tm, nk),
        in_specs=[rows(D_MODEL), rows(512), rows(MW), full(fw["w_out_nsa"]), full(fw["w_out_ml"]), full(fw["g_cross"]),
                  full(fw["w_cq"]), keys, keys, full(fw["w_co"]), full(fw["g_ffn"]), full(fw["w_router"]),
                  full(fw["b_router"])],
        out_specs=[rows(D_MODEL), rows(D_MODEL), rows(LANES)],
        out_shape=[jax.ShapeDtypeStruct((m, D_MODEL), F32), jax.ShapeDtypeStruct((m, D_MODEL), BF16),
                   jax.ShapeDtypeStruct((m, LANES), F32)],
        scratch_shapes=[pltpu.VMEM((tm, D_MODEL), F32), pltpu.VMEM((tm, CH * CD), BF16),
                        pltpu.VMEM((CH, tm, 1), F32), pltpu.VMEM((CH, tm, 1), F32), pltpu.VMEM((tm, CH * CD), F32)],
        compiler_params=pltpu.CompilerParams(dimension_semantics=("arbitrary", "arbitrary"),
                                             vmem_limit_bytes=VMEM_LIMIT),
        name="finish",
    )(x, nsa, ml, fw["w_out_nsa"], fw["w_out_ml"], fw["g_cross"], fw["w_cq"], mk, mv, fw["w_co"], fw["g_ffn"],
      fw["w_router"], fw["b_router"])


def _moe_kernel(h_ref, comb_ref, x2_ref, w13_ref, w2_ref, gfin_ref, y_ref, acc):
    e = pl.program_id(1)

    @pl.when(e == 0)
    def _():
        acc[...] = jnp.zeros(acc.shape, F32)

    au = _dot(h_ref[...], w13_ref[0])
    comb = comb_ref[...]
    ce = jnp.sum(jnp.where(_iota(comb.shape, 1) == e, comb, 0.0), axis=1, keepdims=True)
    hid = _silu(au[:, :D_EXPERT]) * au[:, D_EXPERT:] * ce
    acc[...] += _dot(hid.astype(BF16), w2_ref[0])

    @pl.when(e == pl.num_programs(1) - 1)
    def _():
        y_ref[...] = _rms(x2_ref[...] + acc[...], gfin_ref[...])


def _moe(h3, comb, x2, w13, w2, g_final, tm):
    m = h3.shape[0]
    rows = lambda n: pl.BlockSpec((tm, n), lambda i, e: (i, 0))
    return pl.pallas_call(
        _moe_kernel,
        grid=(m // tm, N_EXPERT),
        in_specs=[rows(D_MODEL), rows(LANES), rows(D_MODEL),
                  pl.BlockSpec((1, D_MODEL, 2 * D_EXPERT), lambda i, e: (e, 0, 0)),
                  pl.BlockSpec((1, D_EXPERT, D_MODEL), lambda i, e: (e, 0, 0)),
                  pl.BlockSpec((1, D_MODEL), lambda i, e: (0, 0))],
        out_specs=rows(D_MODEL),
        out_shape=jax.ShapeDtypeStruct((m, D_MODEL), F32),
        scratch_shapes=[pltpu.VMEM((tm, D_MODEL), F32)],
        compiler_params=pltpu.CompilerParams(dimension_semantics=("arbitrary", "arbitrary"),
                                             vmem_limit_bytes=VMEM_LIMIT),
        name="moe",
    )(h3, comb, x2, w13, w2, g_final)


def _prep_in_weights(g_mix, w_in, b_in):
    w, b = w_in[0], b_in[0]
    o = [0, 512, 1280, 1304, 2840, 2848]
    head_sel = (jnp.arange(NSA_HEADS)[:, None] // GQA == jnp.arange(KVH)[None, :]).astype(F32)

    def pad_q(a):
        a = a.reshape(a.shape[:-1] + (NSA_HEADS, 1, DH)) * head_sel[:, :, None]
        return a.reshape(a.shape[:-3] + (NSA_HEADS * KVH * DH,))

    def pad_lanes(a, n):
        return jnp.pad(a, [(0, 0)] * (a.ndim - 1) + [(0, n - a.shape[-1])])

    wif = pad_lanes(w[:, o[4]:o[5]], LANES)
    bif = pad_lanes(b[o[4]:o[5]], LANES)
    return {
        "g_mix": g_mix.reshape(1, D_MODEL),
        "wq": pad_q(w[:, o[0]:o[1]]).astype(BF16), "bq": pad_q(b[o[0]:o[1]]).reshape(1, -1),
        "wkv": w[:, o[1]:o[2]].astype(BF16), "bkv": b[o[1]:o[2]].reshape(1, -1),
        "wg": pad_lanes(w[:, o[2]:o[3]], LANES).astype(BF16), "bg": pad_lanes(b[o[2]:o[3]], LANES).reshape(1, -1),
        "wm": w[:, o[3]:o[4]].astype(BF16), "bm": b[o[3]:o[4]].reshape(1, -1),
        "wif": wif.astype(BF16), "bif": bif.reshape(1, -1),
        "wift": wif[:, :16].T.astype(BF16), "bift": bif[:16].reshape(16, 1),
    }


def _prep_compress_weights(pe, w1, b1, w2):
    eye = jnp.eye(KVH, dtype=F32)
    wbd = jnp.einsum("pdf,hg->phdgf", w1, eye).reshape(CMP_BLOCK, KVH * DH, KVH * CMP_HIDDEN)
    wa, wb = wbd[:CMP_STRIDE], wbd[CMP_STRIDE:]
    wab = jnp.concatenate([wa, wb], axis=-1)
    wab = wab.reshape(CMP_STRIDE // 2, 2 * KVH * DH, 2 * KVH * CMP_HIDDEN)
    bias = b1 + jnp.einsum("pd,pdf->f", pe, w1, precision=lax.Precision.HIGHEST)
    bias = jnp.tile(bias, KVH).reshape(1, KVH * CMP_HIDDEN)
    w2bd = jnp.einsum("fd,hg->hfgd", w2, eye).reshape(KVH * CMP_HIDDEN, KVH * DH)
    return wab.astype(BF16), bias, w2bd.astype(BF16)


def _sel_matrix(nc, n_blocks_padded):
    c = jnp.arange(nc)[:, None]
    j = jnp.arange(n_blocks_padded)[None, :]
    ok = (c >= 4 * j - 1) & (c <= 4 * j + 3) & (c < nc - 1)
    return ok.astype(BF16)


def _round_up(x, m):
    return (x + m - 1) // m * m


def kernel(x_prompt, x_sample, cache_cmp_k, cache_cmp_v, cache_sel_k, cache_sel_v, cache_win_k, cache_win_v, state_mlstm_c, state_mlstm_n, state_mlstm_m, state_conv, cache_mem_k, cache_mem_v, page_table, mem_prompt, g_mix, w_in, b_in, ck_pe, ck_w1, ck_b1, ck_w2, cv_pe, cv_w1, cv_b1, cv_w2, conv_w, conv_b, m_wq, m_wk, m_norm_g, w_out, g_cross, g_mem, w_cq, w_ck, w_cv, w_co, g_ffn, w_group, b_group, w_expert, b_expert, e_w1, e_w3, e_w2, g_final):
    t = x_prompt.shape[1]
    bsz, n_tok = x_sample.shape[:2]
    n_pages = page_table.shape[1]
    past = n_pages * PAGE
    n_pool = cache_cmp_k.shape[1]
    assert x_prompt.shape[0] == 1 and g_mix.shape[0] == 1 and t % SEL_TILE == 0 and t // SEL_TILE <= 32
    assert n_tok <= 8 and cache_win_k.shape[2] == WINDOW

    inw = _prep_in_weights(g_mix[0], w_in, b_in)
    ka, kb, k2 = _prep_compress_weights(ck_pe[0], ck_w1[0], ck_b1[0], ck_w2[0])
    va, vb, v2 = _prep_compress_weights(cv_pe[0], cv_w1[0], cv_b1[0], cv_w2[0])
    cw = {"wab": jnp.stack([ka, va]), "bias": jnp.stack([kb, vb]), "w2": jnp.stack([k2, v2])}
    mw = {"conv_w": conv_w[0], "conv_b": conv_b[0].reshape(1, MW), "wq": m_wq[0].astype(BF16),
          "wk": m_wk[0].astype(BF16), "wkt": jnp.swapaxes(m_wk[0], 1, 2).astype(BF16),
          "norm_g": m_norm_g[0].reshape(1, MW)}
    wo = w_out[0]
    wo_nsa = wo[:NSA_HEADS * DH].reshape(KVH, GQA, DH, D_MODEL).transpose(1, 0, 2, 3).reshape(NSA_HEADS * DH, D_MODEL)
    w_router = jnp.pad(jnp.concatenate([w_expert[0], w_group[0]], axis=1), ((0, 0), (0, LANES - N_EXPERT - N_GROUPS)))
    b_router = jnp.pad(jnp.concatenate([b_expert[0], b_group[0]]), (0, LANES - N_EXPERT - N_GROUPS)).reshape(1, LANES)
    fw = {"w_out_nsa": wo_nsa.astype(BF16), "w_out_ml": wo[NSA_HEADS * DH:].astype(BF16),
          "g_cross": g_cross[0].reshape(1, -1), "w_cq": w_cq[0].astype(BF16), "w_co": w_co[0].astype(BF16),
          "g_ffn": g_ffn[0].reshape(1, -1), "w_router": w_router.astype(BF16), "b_router": b_router}
    w13 = jnp.concatenate([e_w1[0], e_w3[0]], axis=-1).reshape(N_EXPERT, D_MODEL, 2 * D_EXPERT).astype(BF16)
    w2e = e_w2[0].reshape(N_EXPERT, D_EXPERT, D_MODEL).astype(BF16)
    gfin = g_final.reshape(1, D_MODEL)

    xp = x_prompt[0]
    q_p, kvf_p, kvb_p, gate_p, m3_p, gif_p, gift_p = _in_projection(xp, inw, 512)
    cmp_p = _compress_prompt(kvf_p, cw)
    msel_p = _sel_matrix(t // CMP_STRIDE, _round_up(t // SEL_BLOCK, LANES))
    nsa_p = _nsa_prompt(q_p, gate_p, cmp_p[0].astype(BF16), cmp_p[1].astype(BF16), kvb_p, msel_p)
    lc_p = 256 if t % 256 == 0 else t
    ml_p, c1_p, m1_p = _mlstm(m3_p[None], gif_p[None], gift_p[None], jnp.zeros((1, 8, MW), F32),
                              jnp.zeros((1, MH, MD, 2 * MD), F32), jnp.zeros((1, 1, LANES), F32), mw, lc_p)
    mk_p, mv_p = _mem_kv(mem_prompt[0], g_mem[0].reshape(1, -1), w_ck[0].astype(BF16), w_cv[0].astype(BF16))
    x2_p, h3_p, comb_p = _finish(xp, nsa_p, ml_p[0], mk_p, mv_p, fw, 512, N_MEM, t)
    y_p = _moe(h3_p, comb_p, x2_p, w13, w2e, gfin, 1024)

    kv5 = lambda a: a.reshape(1, 1, -1, KVH, DH)
    wb = min(WINDOW, t)
    prompt_state = (
        kv5(kvf_p[:, 0:128]), kv5(kvf_p[:, 128:256]), kv5(kvf_p[:, 256:384]), kv5(kvf_p[:, 384:512]),
        kv5(kvf_p[t - wb:, 512:640]), kv5(kvf_p[t - wb:, 640:768]),
        c1_p[:, :, :, :MD][None], c1_p[:, :, :, MD][None], m1_p[:, 0, :MH][None],
        m3_p[None, None, t - (CONV_W - 1):, :MW],
        mk_p.reshape(1, 1, N_MEM, CH, CD), mv_p.reshape(1, 1, N_MEM, CH, CD))

    n_s = bsz * n_tok
    xs = x_sample.reshape(n_s, D_MODEL)
    q_s, kvf_s, _, gate_s, m3_s, gif_s, gift_s = _in_projection(xs, inw, n_s)
    q_dec = q_s.reshape(bsz, n_tok, KVH, GQA, LANES).transpose(0, 2, 1, 3, 4)
    q_dec = jnp.pad(q_dec, ((0, 0), (0, 0), (0, 0), (0, 8 - GQA), (0, 0))).reshape(bsz, KVH, 8 * n_tok, LANES)
    g_dec = gate_s[:, :NSA_HEADS * 3].reshape(bsz, n_tok, KVH, GQA, 3).transpose(0, 2, 1, 3, 4)
    g_dec = jnp.pad(g_dec, ((0, 0), (0, 0), (0, 0), (0, 8 - GQA), (0, LANES - 3))).reshape(bsz, KVH, 8 * n_tok, LANES)
    q64 = jnp.stack([q_dec[:, h, :, h * DH:(h + 1) * DH] for h in range(KVH)], axis=1)
    pool_t = lambda a: jnp.transpose(a[0], (0, 2, 3, 1)).reshape(n_pool, KVH * DH, PAGE)
    win_t = lambda a: jnp.transpose(a[0], (0, 2, 3, 1)).reshape(bsz, KVH * DH, WINDOW)
    ns = past // SEL_BLOCK + 1
    msel_s = _sel_matrix(past // CMP_STRIDE, _round_up(ns, LANES))
    o_cmp, idx = _nsa_dec1(page_table, pool_t(cache_cmp_k), pool_t(cache_cmp_v), q_dec, cw, msel_s)
    new_rows = lambda c: jnp.pad(kvf_s[:, c * LANES:(c + 1) * LANES].reshape(bsz, n_tok, LANES),
                                 ((0, 0), (0, 8 - n_tok), (0, 0)))
    idx_flat = idx.reshape(bsz, KVH, 8, LANES)[:, :, :n_tok, :N_SEL].reshape(-1)
    o_dec = _nsa_dec2(page_table, idx_flat, pool_t(cache_sel_k), pool_t(cache_sel_v), q_dec, q64, g_dec, o_cmp,
                      new_rows(2), new_rows(3), win_t(cache_win_k), win_t(cache_win_v), new_rows(4), new_rows(5))
    o_dec = o_dec.reshape(bsz, KVH, n_tok, 8, KVH, DH)[:, :, :, :GQA]
    nsa_s = jnp.stack([o_dec[:, h, :, :, h, :] for h in range(KVH)], axis=3).reshape(n_s, NSA_HEADS * DH)

    pad_tok = lambda a: jnp.pad(a.reshape((bsz, n_tok) + a.shape[1:]), ((0, 0), (0, 8 - n_tok), (0, 0)))
    gif_d = pad_tok(gif_s)
    gift_d = gift_s.reshape(8, bsz, n_tok).transpose(1, 0, 2)
    pad_gate = jnp.where(jnp.arange(8)[:, None] < MH, -1.0e30, 0.0)
    gif_d = jnp.where((jnp.arange(8)[:, None] >= n_tok) & (jnp.arange(LANES)[None, :] < MH), -1.0e30,
                      jnp.where(jnp.arange(8)[:, None] >= n_tok, 0.0, gif_d))
    gift_d = jnp.concatenate([gift_d, jnp.broadcast_to(pad_gate[None], (bsz, 8, 8 - n_tok))], axis=2)
    conv0 = jnp.pad(state_conv[0], ((0, 0), (8 - (CONV_W - 1), 0), (0, 0)))
    c0aug = jnp.concatenate([state_mlstm_c[0], state_mlstm_n[0][..., None],
                             jnp.zeros((bsz, MH, MD, MD - 1), F32)], axis=-1)
    m0 = jnp.pad(state_mlstm_m[0], ((0, 0), (0, LANES - MH))).reshape(bsz, 1, LANES)
    ml_s, c1_s, m1_s = _mlstm(pad_tok(m3_s), gif_d, gift_d, conv0, c0aug, m0, mw, 8)
    ml_s = ml_s[:, :n_tok].reshape(n_s, MW)
    mk_s = cache_mem_k[0].reshape(bsz * N_MEM, CH * CD)
    mv_s = cache_mem_v[0].reshape(bsz * N_MEM, CH * CD)
    x2_s, h3_s, comb_s = _finish(xs, nsa_s.astype(BF16), ml_s, mk_s, mv_s, fw, n_s, 4 * N_MEM, n_tok)
    y_s = _moe(h3_s, comb_s, x2_s, w13, w2e, gfin, n_s)

    new5 = lambda c: kvf_s[:, c * LANES:(c + 1) * LANES].reshape(1, bsz, n_tok, KVH, DH)
    roll_win = lambda old, c: jnp.concatenate([old[:, :, n_tok:], new5(c)], axis=2)
    conv_ext = jnp.concatenate([state_conv[0], m3_s[:, :MW].reshape(bsz, n_tok, MW)], axis=1)
    sample_state = (
        new5(0), new5(1), new5(2), new5(3), roll_win(cache_win_k, 4), roll_win(cache_win_v, 5),
        c1_s[:, :, :, :MD][None], c1_s[:, :, :, MD][None], m1_s[:, 0, :MH][None], conv_ext[None, :, n_tok:])

    return (y_p[None], y_s.reshape(bsz, n_tok, D_MODEL)) + prompt_state + sample_state
```
